```python
import jax
import jax.numpy as jnp
from jax import lax
import numpy as np

D_MODEL = 4096
BATCH = 4
SEQ = 2048
DEPTH = 4
DEC_BATCH = 8
DEC_SEQ = 4
PAST_LEN = 8192
PAGE_SIZE = 128

N_AB = (DEPTH + 1) // 2
N_CD = DEPTH // 2
GROUP_W = D_MODEL // 2
DK_A = 256
DV_A = 256
H_A = GROUP_W // DV_A
RET_CHUNK = 128
ROPE_BASE = 10000.0
DH_B = 128
H_B = GROUP_W // DH_B
MOBA_BLOCK = 256
MOBA_TOPK = 3
MOBA_QCHUNK = 16
W_C = GROUP_W
BW_C = 128
H_C = W_C // BW_C
CONV_W = 4
RG_C = 8.0
DK_D = 128
DV_D = 128
H_D = GROUP_W // DV_D
HGRN_CHUNK = 64
MIX_W = H_A * DV_A + H_B * DH_B
MIX_W_CD = W_C + H_D * DV_D
D_FF = ((8 * D_MODEL // 3 + 255) // 256) * 256
IN_AB_SPLIT = (H_A * DK_A, H_A * DK_A, H_A * DV_A, H_A * DV_A, H_B * DH_B, H_B * DH_B, H_B * DH_B)
IN_CD_SPLIT = (W_C, W_C, H_D * DK_D, H_D * DK_D, H_D * DV_D, H_D * DV_D)
IN_AB = sum(IN_AB_SPLIT)
IN_CD = sum(IN_CD_SPLIT)
NORM_EPS = 1e-6

kernel_name = "hybrid_retnet_moba_rglru_hgrn2_macaron_step"


def _split(x, sizes):
    cuts = [int(c) for c in np.cumsum(sizes)[:-1]]
    return jnp.split(x, cuts, axis=-1)


def _chunks(x, c):
    b, t = x.shape[0], x.shape[1]
    return x.reshape((b, t // c, c) + x.shape[2:]).swapaxes(0, 1)


def _unchunk(x):
    n, b, c = x.shape[0], x.shape[1], x.shape[2]
    return x.swapaxes(0, 1).reshape((b, n * c) + x.shape[3:])


def rmsnorm(x, g):
    xf = x.astype(jnp.float32)
    y = xf * lax.rsqrt(jnp.mean(xf * xf, axis=-1, keepdims=True) + NORM_EPS)
    return (y * g.astype(jnp.float32)).astype(x.dtype)


def swiglu(x, w_gate, w_up, w_down):
    return (jax.nn.silu(x @ w_gate) * (x @ w_up)) @ w_down


def rotary(x, pos):
    half = x.shape[-1] // 2
    freq = ROPE_BASE ** (-jnp.arange(half, dtype=jnp.float32) / half)
    ang = pos.astype(jnp.float32)[:, None] * freq[None, :]
    cos = jnp.cos(ang)[None, :, None, :]
    sin = jnp.sin(ang)[None, :, None, :]
    x1 = x[..., :half].astype(jnp.float32)
    x2 = x[..., half:].astype(jnp.float32)
    return jnp.concatenate([x1 * cos - x2 * sin, x1 * sin + x2 * cos], axis=-1).astype(x.dtype)


def head_layernorm(x, g):
    mu = jnp.mean(x, axis=-1, keepdims=True)
    xc = x - mu
    var = jnp.mean(xc * xc, axis=-1, keepdims=True)
    return xc * lax.rsqrt(var + NORM_EPS) * g


def retention_chunked(q, k, v, s0, chunk):
    n_heads = q.shape[2]
    log_g = jnp.log1p(-jnp.exp2(-5.0 - jnp.arange(n_heads, dtype=jnp.float32)))
    idx = jnp.arange(chunk, dtype=jnp.float32)
    causal = idx[:, None] >= idx[None, :]
    intra = jnp.exp(jnp.where(causal[None], (idx[:, None] - idx[None, :])[None] * log_g[:, None, None], -jnp.inf))
    inter = jnp.exp((idx[:, None] + 1.0) * log_g[None, :])
    to_state = jnp.exp((chunk - 1.0 - idx)[:, None] * log_g[None, :])
    carry_decay = jnp.exp(chunk * log_g)

    def step(s, inp):
        qc, kc, vc = inp
        att = jnp.einsum("bihd,bjhd->bhij", qc, kc) * intra[None]
        o = jnp.einsum("bhij,bjhe->bihe", att, vc) + jnp.einsum("bihd,bhde->bihe", qc, s) * inter[None, :, :, None]
        s_new = carry_decay[None, :, None, None] * s + jnp.einsum("bjhd,bjhe->bhde", kc * to_state[None, :, :, None], vc)
        return s_new, o

    qf = q.astype(jnp.float32)
    kf = k.astype(jnp.float32)
    vf = v.astype(jnp.float32)
    s_last, o = lax.scan(step, s0.astype(jnp.float32), (_chunks(qf, chunk), _chunks(kf, chunk), _chunks(vf, chunk)))
    return _unchunk(o), s_last


def moba_attention(q, k_all, v_all, pos0, q_chunk):
    b, tq, nh, dh = q.shape
    tk = k_all.shape[1]
    nb = -(-tk // MOBA_BLOCK)
    pad = nb * MOBA_BLOCK - tk
    kb = jnp.pad(k_all, ((0, 0), (0, pad), (0, 0), (0, 0))).reshape(b, nb, MOBA_BLOCK, nh, dh).transpose(0, 3, 1, 2, 4)
    vb = jnp.pad(v_all, ((0, 0), (0, pad), (0, 0), (0, 0))).reshape(b, nb, MOBA_BLOCK, nh, dh).transpose(0, 3, 1, 2, 4)
    kmean = jnp.mean(kb, axis=3, dtype=jnp.float32)
    n_sel = min(MOBA_TOPK, nb)
    scale = dh ** -0.5
    bi = jnp.arange(b)[:, None, None, None]
    hi = jnp.arange(nh)[None, None, :, None]
    offs = jnp.arange(MOBA_BLOCK)

    def one_chunk(args):
        qq, pp = args
        blk = pp // MOBA_BLOCK
        gate = jnp.einsum("bqhd,bhnd->bqhn", qq.astype(jnp.float32), kmean)
        past = jnp.arange(nb)[None, :] < blk[:, None]
        gate = jnp.where(past[None, :, None, :], gate, -jnp.inf)
        _, idx = lax.top_k(gate, n_sel)
        valid = idx < blk[None, :, None, None]
        k_sel = kb[bi, hi, idx]
        v_sel = vb[bi, hi, idx]
        s_sel = jnp.einsum("bqhd,bqhsjd->bqhsj", qq, k_sel, preferred_element_type=jnp.float32) * scale
        s_sel = jnp.where(valid[..., None], s_sel, -jnp.inf).reshape(b, q_chunk, nh, n_sel * MOBA_BLOCK)
        k_own = kb[:, :, blk]
        v_own = vb[:, :, blk]
        s_own = jnp.einsum("bqhd,bhqjd->bqhj", qq, k_own, preferred_element_type=jnp.float32) * scale
        causal = (blk[:, None] * MOBA_BLOCK + offs[None, :]) <= pp[:, None]
        s_own = jnp.where(causal[None, :, None, :], s_own, -jnp.inf)
        p = jax.nn.softmax(jnp.concatenate([s_sel, s_own], axis=-1), axis=-1)
        p_sel = p[..., : n_sel * MOBA_BLOCK].reshape(b, q_chunk, nh, n_sel, MOBA_BLOCK).astype(v_sel.dtype)
        p_own = p[..., n_sel * MOBA_BLOCK:].astype(v_own.dtype)
        o = (jnp.einsum("bqhsj,bqhsjd->bqhd", p_sel, v_sel, preferred_element_type=jnp.float32)
             + jnp.einsum("bqhj,bhqjd->bqhd", p_own, v_own, preferred_element_type=jnp.float32))
        return o.astype(q.dtype)

    pos = (pos0 + jnp.arange(tq)).reshape(tq // q_chunk, q_chunk)
    o = lax.map(one_chunk, (_chunks(q, q_chunk), pos))
    return _unchunk(o)


def _lin_combine(left, right):
    a_l, b_l = left
    a_r, b_r = right
    return a_l * a_r, a_r * b_l + b_r


def rglru_block(xb, gb, conv_buf, h0, conv_w, conv_b, w_r, b_r, w_i, b_i, lam):
    b, t, w = xb.shape
    xin = jnp.concatenate([conv_buf.astype(xb.dtype), xb], axis=1)
    xc = conv_b.astype(xb.dtype) + xin[:, 0:t] * conv_w[0]
    for j in range(1, CONV_W):
        xc = xc + xin[:, j:j + t] * conv_w[j]
    new_buf = xin[:, t:]
    xh = xc.reshape(b, t, H_C, BW_C)
    r = jax.nn.sigmoid((jnp.einsum("bthi,hij->bthj", xh, w_r).reshape(b, t, w) + b_r).astype(jnp.float32))
    ig = jax.nn.sigmoid((jnp.einsum("bthi,hij->bthj", xh, w_i).reshape(b, t, w) + b_i).astype(jnp.float32))
    log_a = -RG_C * r * jax.nn.softplus(-lam.astype(jnp.float32))
    a = jnp.exp(log_a)
    u = jnp.sqrt(-jnp.expm1(2.0 * log_a)) * ig * xc.astype(jnp.float32)
    u = u.at[:, 0].add(a[:, 0] * h0.astype(jnp.float32))
    _, hseq = lax.associative_scan(_lin_combine, (a, u), axis=1)
    out = hseq.astype(gb.dtype) * jax.nn.gelu(gb)
    return out, new_buf, hseq[:, -1]


def gla_chunked(q, k, log_f, v, s0, chunk):
    causal = jnp.tril(jnp.ones((chunk, chunk), dtype=bool))

    def step(s, inp):
        qc, kc, lc, vc = inp
        cum = jnp.cumsum(lc, axis=1)
        o_inter = jnp.einsum("bihd,bhde->bihe", qc * jnp.exp(cum), s)
        diff = cum[:, :, None] - cum[:, None, :]
        dec = jnp.exp(jnp.where(causal[None, :, :, None, None], diff, -jnp.inf))
        att = jnp.einsum("bihd,bijhd,bjhd->bhij", qc, dec, kc)
        o_intra = jnp.einsum("bhij,bjhe->bihe", att, vc)
        last = cum[:, -1]
        s_new = jnp.exp(last)[..., None] * s + jnp.einsum("bjhd,bjhe->bhde", kc * jnp.exp(last[:, None] - cum), vc)
        return s_new, o_inter + o_intra

    s_last, o = lax.scan(step, s0, (_chunks(q, chunk), _chunks(k, chunk), _chunks(log_f, chunk), _chunks(v, chunk)))
    return _unchunk(o), s_last


def hgrn2_block(q, f_logit, i_in, g, s0, lb, norm_g, chunk):
    b, t = q.shape[0], q.shape[1]
    qf = q.reshape(b, t, H_D, DK_D).astype(jnp.float32)
    fz = f_logit.reshape(b, t, H_D, DK_D).astype(jnp.float32)
    lb = lb.reshape(H_D, DK_D)
    f = lb + (1.0 - lb) * jax.nn.sigmoid(fz)
    log_f = jnp.log(f)
    key = (1.0 - lb) * jax.nn.sigmoid(-fz)
    vf = i_in.reshape(b, t, H_D, DV_D).astype(jnp.float32)
    o, s_new = gla_chunked(qf, key, log_f, vf, s0.astype(jnp.float32), chunk)
    o = o * lax.rsqrt(jnp.mean(o * o, axis=-1, keepdims=True) + NORM_EPS) * norm_g.astype(jnp.float32).reshape(H_D, DV_D)
    return o.reshape(b, t, H_D * DV_D).astype(g.dtype) * jax.nn.silu(g), s_new


def setup_inputs(seed: int = 0) -> dict:
    key = jax.random.key(seed)
    ks = jax.random.split(key, 32)

    def nrm(i, shape, scale):
        return scale * jax.random.normal(ks[i], shape, jnp.float32)

    def gain(i, shape):
        return 1.0 + nrm(i, shape, 0.02)

    n_pages = PAST_LEN // PAGE_SIZE
    n_used = DEC_BATCH * n_pages
    n_pool = n_used + max(1, n_used // 4)
    page_table = jax.random.permutation(ks[0], n_pool)[:n_used].reshape(DEC_BATCH, n_pages).astype(jnp.int32)
    a_c = jax.random.uniform(ks[1], (N_CD, W_C), jnp.float32, 0.9, 0.999)
    a_base = a_c ** (1.0 / RG_C)
    lru_lambda = jnp.log(a_base) - jnp.log1p(-a_base)
    return {
        "x_prompt": nrm(2, (BATCH, SEQ, D_MODEL), 1.0),
        "x_sample": nrm(3, (DEC_BATCH, DEC_SEQ, D_MODEL), 1.0),
        "state_ret": nrm(4, (N_AB, DEC_BATCH, H_A, DK_A, DV_A), 0.1),
        "cache_k": nrm(5, (N_AB, n_pool, PAGE_SIZE, H_B, DH_B), 1.0),
        "cache_v": nrm(6, (N_AB, n_pool, PAGE_SIZE, H_B, DH_B), 1.0),
        "state_rglru": nrm(7, (N_CD, DEC_BATCH, W_C), 0.5),
        "state_conv": nrm(8, (N_CD, DEC_BATCH, CONV_W - 1, W_C), 1.0),
        "state_hgrn": nrm(9, (N_CD, DEC_BATCH, H_D, DK_D, DV_D), 0.5),
        "page_table": page_table,
        "norm_ffn1": gain(10, (DEPTH, D_MODEL)),
        "norm_mix": gain(11, (DEPTH, D_MODEL)),
        "norm_ffn2": gain(12, (DEPTH, D_MODEL)),
        "ffn_gate": nrm(13, (DEPTH, 2, D_MODEL, D_FF), D_MODEL ** -0.5),
        "ffn_up": nrm(14, (DEPTH, 2, D_MODEL, D_FF), D_MODEL ** -0.5),
        "ffn_down": nrm(15, (DEPTH, 2, D_FF, D_MODEL), D_FF ** -0.5),
        "w_in_ab": nrm(16, (N_AB, D_MODEL, IN_AB), D_MODEL ** -0.5),
        "w_out_ab": nrm(17, (N_AB, MIX_W, D_MODEL), MIX_W ** -0.5),
        "gn_ret": gain(18, (N_AB, H_A * DV_A)),
        "w_in_cd": nrm(19, (N_CD, D_MODEL, IN_CD), D_MODEL ** -0.5),
        "w_out_cd": nrm(20, (N_CD, MIX_W_CD, D_MODEL), MIX_W_CD ** -0.5),
        "conv_w": nrm(21, (N_CD, CONV_W, W_C), CONV_W ** -0.5),
        "conv_b": nrm(22, (N_CD, W_C), 0.01),
        "w_rgate": nrm(23, (N_CD, H_C, BW_C, BW_C), BW_C ** -0.5),
        "b_rgate": nrm(24, (N_CD, W_C), 0.01),
        "w_igate": nrm(25, (N_CD, H_C, BW_C, BW_C), BW_C ** -0.5),
        "b_igate": nrm(26, (N_CD, W_C), 0.01),
        "lru_lambda": lru_lambda,
        "hgrn_lb_logits": nrm(27, (N_CD, H_D * DK_D), 1.0),
        "hgrn_norm": gain(28, (N_CD, H_D * DV_D)),
        "final_norm": gain(29, (D_MODEL,)),
    }


def reference(x_prompt, x_sample, state_ret, cache_k, cache_v, state_rglru, state_conv, state_hgrn, page_table,
              norm_ffn1, norm_mix, norm_ffn2, ffn_gate, ffn_up, ffn_down,
              w_in_ab, w_out_ab, gn_ret, w_in_cd, w_out_cd, conv_w, conv_b,
              w_rgate, b_rgate, w_igate, b_igate, lru_lambda, hgrn_lb_logits, hgrn_norm, final_norm):
    lb_soft = jax.nn.softmax(hgrn_lb_logits.astype(jnp.float32), axis=0)
    lower_bounds = jnp.cumsum(lb_soft, axis=0) - lb_soft[0:1]

    def forward(x, pos, ret_s, k_past, v_past, lru_h, conv_buf, hgrn_s, ret_chunk, moba_chunk, hgrn_chunk):
        nbat, t = x.shape[0], x.shape[1]
        rets, ks, vs, hs, bufs, hgs = [], [], [], [], [], []
        for l in range(DEPTH):
            i = l // 2
            x = x + 0.5 * swiglu(rmsnorm(x, norm_ffn1[l]), ffn_gate[l, 0], ffn_up[l, 0], ffn_down[l, 0])
            h = rmsnorm(x, norm_mix[l])
            if l % 2 == 0:
                qa, ka, va, ga, qb, kb, vb = _split(h @ w_in_ab[i], IN_AB_SPLIT)
                qa = rotary(qa.reshape(nbat, t, H_A, DK_A), pos)
                ka = rotary(ka.reshape(nbat, t, H_A, DK_A), pos) * (DK_A ** -0.5)
                oa, s_new = retention_chunked(qa, ka, va.reshape(nbat, t, H_A, DV_A), ret_s[i], ret_chunk)
                oa = head_layernorm(oa, gn_ret[i].astype(jnp.float32).reshape(H_A, DV_A)).reshape(nbat, t, H_A * DV_A)
                oa = oa.astype(x.dtype) * jax.nn.silu(ga)
                kb = kb.reshape(nbat, t, H_B, DH_B)
                vb = vb.reshape(nbat, t, H_B, DH_B)
                ob = moba_attention(qb.reshape(nbat, t, H_B, DH_B),
                                    jnp.concatenate([k_past[i].astype(kb.dtype), kb], axis=1),
                                    jnp.concatenate([v_past[i].astype(vb.dtype), vb], axis=1),
                                    k_past.shape[2], moba_chunk)
                out = jnp.concatenate([oa, ob.reshape(nbat, t, H_B * DH_B)], axis=-1) @ w_out_ab[i]
                rets.append(s_new)
                ks.append(kb)
                vs.append(vb)
            else:
                xc, gc, qd, fd, idd, gd = _split(h @ w_in_cd[i], IN_CD_SPLIT)
                oc, buf, h_last = rglru_block(xc, gc, conv_buf[i], lru_h[i], conv_w[i], conv_b[i],
                                              w_rgate[i], b_rgate[i], w_igate[i], b_igate[i], lru_lambda[i])
                od, hg_new = hgrn2_block(qd, fd, idd, gd, hgrn_s[i], lower_bounds[i], hgrn_norm[i], hgrn_chunk)
                out = jnp.concatenate([oc, od], axis=-1) @ w_out_cd[i]
                hs.append(h_last)
                bufs.append(buf)
                hgs.append(hg_new)
            x = x + out
            x = x + 0.5 * swiglu(rmsnorm(x, norm_ffn2[l]), ffn_gate[l, 1], ffn_up[l, 1], ffn_down[l, 1])
        y = rmsnorm(x, final_norm)
        return y, jnp.stack(rets), jnp.stack(ks), jnp.stack(vs), jnp.stack(hs), jnp.stack(bufs), jnp.stack(hgs)

    bp, tp = x_prompt.shape[0], x_prompt.shape[1]
    y_p, ret_p, k_p, v_p, h_p, buf_p, hg_p = forward(
        x_prompt, jnp.arange(tp),
        jnp.zeros((N_AB, bp, H_A, DK_A, DV_A), jnp.float32),
        jnp.zeros((N_AB, bp, 0, H_B, DH_B), x_prompt.dtype),
        jnp.zeros((N_AB, bp, 0, H_B, DH_B), x_prompt.dtype),
        jnp.zeros((N_CD, bp, W_C), jnp.float32),
        jnp.zeros((N_CD, bp, CONV_W - 1, W_C), x_prompt.dtype),
        jnp.zeros((N_CD, bp, H_D, DK_D, DV_D), jnp.float32),
        RET_CHUNK, MOBA_QCHUNK, HGRN_CHUNK)

    db, n_pages = page_table.shape[0], page_table.shape[1]
    past_len = n_pages * cache_k.shape[2]
    k_past = cache_k[:, page_table].reshape(N_AB, db, past_len, H_B, DH_B)
    v_past = cache_v[:, page_table].reshape(N_AB, db, past_len, H_B, DH_B)
    ts = x_sample.shape[1]
    y_s, ret_s, k_s, v_s, h_s, buf_s, hg_s = forward(
        x_sample, past_len + jnp.arange(ts), state_ret, k_past, v_past,
        state_rglru, state_conv, state_hgrn, ts, ts, ts)

    return (y_p, y_s,
            ret_p.astype(state_ret.dtype), k_p.astype(cache_k.dtype), v_p.astype(cache_v.dtype),
            h_p.astype(state_rglru.dtype), buf_p.astype(state_conv.dtype), hg_p.astype(state_hgrn.dtype),
            ret_s.astype(state_ret.dtype), k_s.astype(cache_k.dtype), v_s.astype(cache_v.dtype),
            h_s.astype(state_rglru.dtype), buf_s.astype(state_conv.dtype), hg_s.astype(state_hgrn.dtype))
```

```python
import functools

import jax
import jax.numpy as jnp
from jax import lax
from jax.experimental import pallas as pl
from jax.experimental.pallas import tpu as pltpu

F32 = jnp.float32
BF16 = jnp.bfloat16

NORM_EPS = 1e-6
ROPE_BASE = 10000.0
MOBA_BLOCK = 256
MOBA_TOPK = 3
RG_C = 8.0
GATE_BLOCK = 128
SUB = 16
MIB = 1024 * 1024

_NT = (((1,), (1,)), ((), ()))
_TN = (((0,), (0,)), ((), ()))


def _params(semantics, vmem_mib):
    return pltpu.CompilerParams(dimension_semantics=semantics,
                                vmem_limit_bytes=int(vmem_mib * MIB))


def _bdot(a, b):
    return jnp.dot(a.astype(BF16), b.astype(BF16), preferred_element_type=F32)


def _bdot_nt(a, b):
    return lax.dot_general(a.astype(BF16), b.astype(BF16), _NT, preferred_element_type=F32)


def _bdot_tn(a, b):
    return lax.dot_general(a.astype(BF16), b.astype(BF16), _TN, preferred_element_type=F32)


def _silu(x):
    return x * jax.nn.sigmoid(x)


def _rms_kernel(x_ref, g_ref, o_ref):
    x = x_ref[...]
    ms = jnp.mean(x * x, axis=-1, keepdims=True)
    o_ref[...] = (x * lax.rsqrt(ms + NORM_EPS) * g_ref[...]).astype(o_ref.dtype)


def rmsnorm(x, g, out_dtype):
    m, d = x.shape
    bm = min(m, 256)
    return pl.pallas_call(
        _rms_kernel,
        grid=(m // bm,),
        in_specs=[pl.BlockSpec((bm, d), lambda i: (i, 0)),
                  pl.BlockSpec((1, d), lambda i: (0, 0))],
        out_specs=pl.BlockSpec((bm, d), lambda i: (i, 0)),
        out_shape=jax.ShapeDtypeStruct((m, d), out_dtype),
        compiler_params=_params(("parallel",), 32),
        name="rmsnorm",
    )(x, g.reshape(1, d))


def _mm_kernel(x_ref, w_ref, o_ref):
    o_ref[...] = jnp.dot(x_ref[...], w_ref[...], preferred_element_type=F32).astype(o_ref.dtype)


def matmul(x, w, out_dtype, bn=512):
    m, k = x.shape
    n = w.shape[1]
    bm = min(m, 1024)
    return pl.pallas_call(
        _mm_kernel,
        grid=(m // bm, pl.cdiv(n, bn)),
        in_specs=[pl.BlockSpec((bm, k), lambda i, j: (i, 0)),
                  pl.BlockSpec((k, bn), lambda i, j: (0, j))],
        out_specs=pl.BlockSpec((bm, bn), lambda i, j: (i, j)),
        out_shape=jax.ShapeDtypeStruct((m, n), out_dtype),
        compiler_params=_params(("parallel", "arbitrary"), 48),
        name="in_proj",
    )(x, w)


def _gate_up_kernel(x_ref, wg_ref, wu_ref, o_ref):
    x = x_ref[...]
    g = jnp.dot(x, wg_ref[...], preferred_element_type=F32)
    u = jnp.dot(x, wu_ref[...], preferred_element_type=F32)
    o_ref[...] = (_silu(g) * u).astype(o_ref.dtype)


def gate_up(x, wg, wu, bn=512):
    m, k = x.shape
    n = wg.shape[1]
    bm = min(m, 1024)
    return pl.pallas_call(
        _gate_up_kernel,
        grid=(m // bm, pl.cdiv(n, bn)),
        in_specs=[pl.BlockSpec((bm, k), lambda i, j: (i, 0)),
                  pl.BlockSpec((k, bn), lambda i, j: (0, j)),
                  pl.BlockSpec((k, bn), lambda i, j: (0, j))],
        out_specs=pl.BlockSpec((bm, bn), lambda i, j: (i, j)),
        out_shape=jax.ShapeDtypeStruct((m, n), BF16),
        compiler_params=_params(("parallel", "arbitrary"), 52),
        name="ffn_gate_up",
    )(x, wg, wu)


def _down_kernel(h_ref, w_ref, x_ref, o_ref, *, scale):
    part = scale * jnp.dot(h_ref[...], w_ref[...], preferred_element_type=F32)

    @pl.when(pl.program_id(2) == 0)
    def _():
        o_ref[...] = x_ref[...] + part

    @pl.when(pl.program_id(2) > 0)
    def _():
        o_ref[...] += part


def down_residual(h, w, x, scale, bn=512):
    m, k = h.shape
    n = w.shape[1]
    bm = min(m, 1024)
    nk = 2
    tk = k // nk
    assert tk * nk == k and tk % 128 == 0
    return pl.pallas_call(
        functools.partial(_down_kernel, scale=scale),
        grid=(m // bm, n // bn, nk),
        in_specs=[pl.BlockSpec((bm, tk), lambda i, j, kk: (i, kk)),
                  pl.BlockSpec((tk, bn), lambda i, j, kk: (kk, j)),
                  pl.BlockSpec((bm, bn), lambda i, j, kk: (i, j))],
        out_specs=pl.BlockSpec((bm, bn), lambda i, j, kk: (i, j)),
        out_shape=jax.ShapeDtypeStruct((m, n), F32),
        compiler_params=_params(("parallel", "arbitrary", "arbitrary"), 52),
        name="ffn_down",
    )(h, w, x)


def _out_proj_kernel(a_ref, b_ref, wa_ref, wb_ref, x_ref, o_ref):
    acc = jnp.dot(a_ref[...], wa_ref[...], preferred_element_type=F32)
    acc = acc + jnp.dot(b_ref[...], wb_ref[...], preferred_element_type=F32)
    o_ref[...] = x_ref[...] + acc


def out_proj_residual(a, b, w, x, bn=512):
    m, ka = a.shape
    kb = b.shape[1]
    assert ka == kb and w.shape[0] == ka + kb
    n = w.shape[1]
    bm = min(m, 1024)
    return pl.pallas_call(
        _out_proj_kernel,
        grid=(m // bm, n // bn),
        in_specs=[pl.BlockSpec((bm, ka), lambda i, j: (i, 0)),
                  pl.BlockSpec((bm, kb), lambda i, j: (i, 0)),
                  pl.BlockSpec((ka, bn), lambda i, j: (0, j)),
                  pl.BlockSpec((kb, bn), lambda i, j: (1, j)),
                  pl.BlockSpec((bm, bn), lambda i, j: (i, j))],
        out_specs=pl.BlockSpec((bm, bn), lambda i, j: (i, j)),
        out_shape=jax.ShapeDtypeStruct((m, n), F32),
        compiler_params=_params(("parallel", "arbitrary"), 48),
        name="out_proj",
    )(a, b, w, w, x)


def _cumsum_rows(x):
    n = x.shape[0]
    row = lax.broadcasted_iota(jnp.int32, x.shape, 0)
    s = 1
    while s < n:
        x = x + jnp.where(row >= s, pltpu.roll(x, s, 0), 0.0)
        s *= 2
    return x


def _linear_scan_rows(a, u):
    n = a.shape[0]
    row = lax.broadcasted_iota(jnp.int32, a.shape, 0)
    s = 1
    while s < n:
        keep = row >= s
        a_prev = jnp.where(keep, pltpu.roll(a, s, 0), 1.0)
        u_prev = jnp.where(keep, pltpu.roll(u, s, 0), 0.0)
        u = a * u_prev + u
        a = a * a_prev
        s *= 2
    return a, u


def _retention_kernel(q_ref, k_ref, v_ref, g_ref, cos_ref, sin_ref, lg_ref, gn_ref, s0_ref,
                      o_ref, sout_ref, s_scr, dec_scr, *, chunk, n_valid, dk):
    c = pl.program_id(2)
    log_g = lg_ref[0][:, :1]

    @pl.when(c == 0)
    def _():
        s_scr[...] = s0_ref[0, 0]
        i = lax.broadcasted_iota(jnp.int32, (chunk, chunk), 0)
        j = lax.broadcasted_iota(jnp.int32, (chunk, chunk), 1)
        d = (i - j).astype(F32) * log_g
        dec_scr[...] = jnp.exp(jnp.where(i >= j, d, -jnp.inf))

    half = dk // 2
    cos = cos_ref[...]
    sin = sin_ref[...]

    def rot(x):
        x1 = x[:, :half]
        x2 = x[:, half:]
        return jnp.concatenate([x1 * cos - x2 * sin, x1 * sin + x2 * cos], axis=-1)

    q = rot(q_ref[0])
    k = rot(k_ref[0]) * (dk ** -0.5)
    v = v_ref[0]
    idx = lax.broadcasted_iota(jnp.int32, (chunk, 1), 0).astype(F32)
    inter = jnp.exp((idx + 1.0) * log_g)
    to_state = jnp.exp((n_valid - 1.0 - idx) * log_g)
    carry = jnp.exp(n_valid * log_g)

    s = s_scr[...]
    att = _bdot_nt(q, k) * dec_scr[...]
    o = _bdot(att, v) + _bdot(q, s) * inter
    s_new = carry * s + _bdot_tn(k * to_state, v)
    s_scr[...] = s_new

    mu = jnp.mean(o, axis=-1, keepdims=True)
    oc = o - mu
    var = jnp.mean(oc * oc, axis=-1, keepdims=True)
    y = oc * lax.rsqrt(var + NORM_EPS) * gn_ref[...]
    o_ref[0] = (y * _silu(g_ref[0])).astype(o_ref.dtype)

    @pl.when(c == pl.num_programs(2) - 1)
    def _():
        sout_ref[0, 0] = s_new


def retention(proj, cos, sin, gn, s0, *, n_heads, dk, dv, n_valid, chunk):
    b, t, _ = proj.shape
    assert dk == dv and t % chunk == 0 and (n_valid == chunk or t == chunk)
    h = n_heads
    log_g = jnp.log1p(-jnp.exp2(-5.0 - jnp.arange(h, dtype=F32)))
    lg = jnp.broadcast_to(log_g[:, None, None], (h, 1, 128))
    kern = functools.partial(_retention_kernel, chunk=chunk, n_valid=n_valid, dk=dk)
    col = lambda off: pl.BlockSpec((1, chunk, dk), lambda bi, hi, ci, off=off: (bi, ci, off + hi))
    return pl.pallas_call(
        kern,
        grid=(b, h, t // chunk),
        in_specs=[col(0), col(h), col(2 * h), col(3 * h),
                  pl.BlockSpec((chunk, dk // 2), lambda bi, hi, ci: (ci, 0)),
                  pl.BlockSpec((chunk, dk // 2), lambda bi, hi, ci: (ci, 0)),
                  pl.BlockSpec((1, 1, 128), lambda bi, hi, ci: (hi, 0, 0)),
                  pl.BlockSpec((1, dv), lambda bi, hi, ci: (0, hi)),
                  pl.BlockSpec((1, 1, dk, dv), lambda bi, hi, ci: (bi, hi, 0, 0))],
        out_specs=[pl.BlockSpec((1, chunk, dv), lambda bi, hi, ci: (bi, ci, hi)),
                   pl.BlockSpec((1, 1, dk, dv), lambda bi, hi, ci: (bi, hi, 0, 0))],
        out_shape=[jax.ShapeDtypeStruct((b, t, h * dv), BF16),
                   jax.ShapeDtypeStruct((b, h, dk, dv), F32)],
        scratch_shapes=[pltpu.VMEM((dk, dv), F32), pltpu.VMEM((chunk, chunk), F32)],
        compiler_params=_params(("parallel", "parallel", "arbitrary"), 32),
        name="retention",
    )(proj, proj, proj, proj, cos, sin, lg, gn.reshape(1, h * dv), s0)


def _topk_select(gate, col, valid):
    nb = gate.shape[1]
    rank = jnp.zeros(gate.shape, jnp.int32)
    for m in range(nb):
        gm = gate[:, m:m + 1]
        beats = (gm > gate) | ((gm == gate) & (col > m))
        rank = rank + beats.astype(jnp.int32)
    return jnp.where((rank < MOBA_TOPK) & valid, 1.0, 0.0)


def _moba_prompt_kernel(q_ref, k_ref, v_ref, o_ref, kmean_scr, kb_scr, vb_scr, *, nb, scale):
    i = pl.program_id(2)
    blk = MOBA_BLOCK

    @pl.when(i == 0)
    def _():
        kk = k_ref[0]
        kmean_scr[...] = jnp.mean(kk.reshape(nb, blk, kk.shape[-1]), axis=1)
        kb_scr[...] = kk.astype(BF16)
        vb_scr[...] = v_ref[0].astype(BF16)

    q = q_ref[0]
    gate = lax.dot_general(q, kmean_scr[...], _NT, precision=lax.Precision.HIGHEST,
                           preferred_element_type=F32)
    col = lax.broadcasted_iota(jnp.int32, (blk, nb), 1)
    past = col < i
    gate = jnp.where(past, gate, -jnp.inf)
    sel = _topk_select(gate, col, past)

    s = _bdot_nt(q, kb_scr[...]) * scale
    row = lax.broadcasted_iota(jnp.int32, (blk, blk), 0)
    colk = lax.broadcasted_iota(jnp.int32, (blk, blk), 1)
    causal = colk <= row
    pieces = []
    for n in range(nb):
        own = jnp.full((blk, blk), n, jnp.int32) == i
        allowed = (jnp.broadcast_to(sel[:, n:n + 1], (blk, blk)) > 0.5) | (own & causal)
        pieces.append(jnp.where(allowed, s[:, n * blk:(n + 1) * blk], -jnp.inf))
    s = jnp.concatenate(pieces, axis=-1)
    m = jnp.max(s, axis=-1, keepdims=True)
    p = jnp.exp(s - m)
    l = jnp.sum(p, axis=-1, keepdims=True)
    o = _bdot(p, vb_scr[...]) / l
    o_ref[0] = o.astype(o_ref.dtype)


def moba_prompt(proj, *, n_heads, dh, q_off, k_off, v_off):
    b, t, _ = proj.shape
    assert t % MOBA_BLOCK == 0
    nb = t // MOBA_BLOCK
    kern = functools.partial(_moba_prompt_kernel, nb=nb, scale=dh ** -0.5)
    return pl.pallas_call(
        kern,
        grid=(b, n_heads, nb),
        in_specs=[pl.BlockSpec((1, MOBA_BLOCK, dh), lambda bi, hi, ii: (bi, ii, q_off + hi)),
                  pl.BlockSpec((1, t, dh), lambda bi, hi, ii: (bi, 0, k_off + hi)),
                  pl.BlockSpec((1, t, dh), lambda bi, hi, ii: (bi, 0, v_off + hi))],
        out_specs=pl.BlockSpec((1, MOBA_BLOCK, dh), lambda bi, hi, ii: (bi, ii, hi)),
        out_shape=jax.ShapeDtypeStruct((b, t, n_heads * dh), BF16),
        scratch_shapes=[pltpu.VMEM((nb, dh), F32), pltpu.VMEM((t, dh), BF16), pltpu.VMEM((t, dh), BF16)],
        compiler_params=_params(("parallel", "parallel", "arbitrary"), 40),
        name="moba_prompt",
    )(proj, proj, proj)


def _moba_scores_kernel(pt_ref, q_ref, k_ref, s_ref, kmean_ref, *, pages_per_block, scale):
    p = pl.program_id(1)

    @pl.when(p == 0)
    def _():
        kmean_ref[...] = jnp.zeros(kmean_ref.shape, F32)

    kp = k_ref[...]
    s_ref[0] = _bdot_nt(q_ref[0], kp) * scale
    ksum = jnp.sum(kp, axis=0, keepdims=True) * (1.0 / MOBA_BLOCK)
    kmean_ref[0, pl.ds(p // pages_per_block, 1), :] += ksum


def _moba_probs_kernel(q_ref, s_ref, kmean_ref, knew_ref, pp_ref, po_ref, *, nblk, n_heads, ts, scale):
    q = q_ref[0]
    rows = q.shape[0]
    gate = lax.dot_general(q, kmean_ref[0], _NT, precision=lax.Precision.HIGHEST,
                           preferred_element_type=F32)
    col = lax.broadcasted_iota(jnp.int32, (rows, nblk), 1)
    sel = _topk_select(gate, col, col >= 0)
    s = s_ref[0]
    pieces = []
    for n in range(nblk):
        keep = jnp.broadcast_to(sel[:, n:n + 1], (rows, MOBA_BLOCK)) > 0.5
        pieces.append(jnp.where(keep, s[:, n * MOBA_BLOCK:(n + 1) * MOBA_BLOCK], -jnp.inf))
    s = jnp.concatenate(pieces, axis=-1)
    s_own = _bdot_nt(q, knew_ref[0]) * scale
    qi = lax.broadcasted_iota(jnp.int32, s_own.shape, 0) // n_heads
    kj = lax.broadcasted_iota(jnp.int32, s_own.shape, 1)
    s_own = jnp.where((kj <= qi) & (kj < ts), s_own, -jnp.inf)
    m = jnp.maximum(jnp.max(s, axis=-1, keepdims=True), jnp.max(s_own, axis=-1, keepdims=True))
    p = jnp.exp(s - m)
    p_own = jnp.exp(s_own - m)
    inv = 1.0 / (jnp.sum(p, axis=-1, keepdims=True) + jnp.sum(p_own, axis=-1, keepdims=True))
    pp_ref[0] = p * inv
    po_ref[0] = p_own * inv


def _moba_pv_kernel(pt_ref, pp_ref, v_ref, po_ref, vnew_ref, o_ref, acc_scr, *, n_heads, dh, ts):
    p = pl.program_id(1)

    @pl.when(p == 0)
    def _():
        acc_scr[...] = _bdot(po_ref[0], vnew_ref[0])

    acc_scr[...] += _bdot(pp_ref[0], v_ref[...])

    @pl.when(p == pl.num_programs(1) - 1)
    def _():
        w = n_heads * dh
        hrow = lax.broadcasted_iota(jnp.int32, (n_heads, w), 0)
        hcol = lax.broadcasted_iota(jnp.int32, (n_heads, w), 1) // dh
        diag = hrow == hcol
        for t in range(ts):
            a = acc_scr[t * n_heads:(t + 1) * n_heads, :]
            o_ref[0, t:t + 1, :] = jnp.sum(jnp.where(diag, a, 0.0), axis=0, keepdims=True).astype(o_ref.dtype)


def moba_sample(qb, knew, vnew, cache_k, cache_v, layer, page_table, *, n_heads, dh):
    b, ts, w = qb.shape
    n_pages = page_table.shape[1]
    page = cache_k.shape[2]
    past = n_pages * page
    assert past % MOBA_BLOCK == 0 and MOBA_BLOCK % page == 0 and ts <= MOBA_BLOCK and w == n_heads * dh
    ppb = MOBA_BLOCK // page
    nblk = past // MOBA_BLOCK
    rows = ts * n_heads
    ts_pad = -(-ts // 16) * 16
    scale = dh ** -0.5
    eye = jnp.eye(n_heads, dtype=qb.dtype)
    qbd = (qb.reshape(b, ts, n_heads, 1, dh) * eye[None, None, :, :, None]).reshape(b, rows, w)
    pad = ((0, 0), (0, ts_pad - ts), (0, 0))
    knew = jnp.pad(knew, pad)
    vnew = jnp.pad(vnew, pad)
    ck = cache_k.reshape(cache_k.shape[0], cache_k.shape[1], page, w)
    cv = cache_v.reshape(cache_v.shape[0], cache_v.shape[1], page, w)

    scores, kmean = pl.pallas_call(
        functools.partial(_moba_scores_kernel, pages_per_block=ppb, scale=scale),
        grid_spec=pltpu.PrefetchScalarGridSpec(
            num_scalar_prefetch=1,
            grid=(b, n_pages),
            in_specs=[pl.BlockSpec((1, rows, w), lambda bi, pi, pt: (bi, 0, 0)),
                      pl.BlockSpec((None, None, page, w), lambda bi, pi, pt: (layer, pt[bi, pi], 0, 0))],
            out_specs=[pl.BlockSpec((1, rows, page), lambda bi, pi, pt: (bi, 0, pi)),
                       pl.BlockSpec((1, nblk, w), lambda bi, pi, pt: (bi, 0, 0))]),
        out_shape=[jax.ShapeDtypeStruct((b, rows, past), F32),
                   jax.ShapeDtypeStruct((b, nblk, w), F32)],
        compiler_params=_params(("parallel", "arbitrary"), 32),
        name="moba_sample_scores",
    )(page_table, qbd, ck)

    p_past, p_own = pl.pallas_call(
        functools.partial(_moba_probs_kernel, nblk=nblk, n_heads=n_heads, ts=ts, scale=scale),
        grid=(b,),
        in_specs=[pl.BlockSpec((1, rows, w), lambda bi: (bi, 0, 0)),
                  pl.BlockSpec((1, rows, past), lambda bi: (bi, 0, 0)),
                  pl.BlockSpec((1, nblk, w), lambda bi: (bi, 0, 0)),
                  pl.BlockSpec((1, ts_pad, w), lambda bi: (bi, 0, 0))],
        out_specs=[pl.BlockSpec((1, rows, past), lambda bi: (bi, 0, 0)),
                   pl.BlockSpec((1, rows, ts_pad), lambda bi: (bi, 0, 0))],
        out_shape=[jax.ShapeDtypeStruct((b, rows, past), F32),
                   jax.ShapeDtypeStruct((b, rows, ts_pad), F32)],
        compiler_params=_params(("parallel",), 40),
        name="moba_sample_probs",
    )(qbd, scores, kmean, knew)

    return pl.pallas_call(
        functools.partial(_moba_pv_kernel, n_heads=n_heads, dh=dh, ts=ts),
        grid_spec=pltpu.PrefetchScalarGridSpec(
            num_scalar_prefetch=1,
            grid=(b, n_pages),
            in_specs=[pl.BlockSpec((1, rows, page), lambda bi, pi, pt: (bi, 0, pi)),
                      pl.BlockSpec((None, None, page, w), lambda bi, pi, pt: (layer, pt[bi, pi], 0, 0)),
                      pl.BlockSpec((1, rows, ts_pad), lambda bi, pi, pt: (bi, 0, 0)),
                      pl.BlockSpec((1, ts_pad, w), lambda bi, pi, pt: (bi, 0, 0))],
            out_specs=pl.BlockSpec((1, ts, w), lambda bi, pi, pt: (bi, 0, 0)),
            scratch_shapes=[pltpu.VMEM((rows, w), F32)]),
        out_shape=jax.ShapeDtypeStruct((b, ts, w), F32),
        compiler_params=_params(("parallel", "arbitrary"), 32),
        name="moba_sample_pv",
    )(page_table, p_past, cv, p_own, vnew)


def _rglru_kernel(x_ref, g_ref, tail0_ref, h0_ref, cw_ref, cb_ref, wr_ref, br_ref, wi_ref, bi_ref, lam_ref,
                  o_ref, hlast_ref, tail_scr, h_scr, *, tc, n_valid):
    c = pl.program_id(2)

    @pl.when(c == 0)
    def _():
        tail_scr[...] = tail0_ref[0]
        h_scr[...] = h0_ref[0]

    x = x_ref[0]
    cw = cw_ref[...]
    taps = cw.shape[0]
    ext = jnp.concatenate([tail_scr[...], x], axis=0)
    xc = cb_ref[...] + x * cw[taps - 1:taps, :]
    for s in range(1, taps):
        xc = xc + pltpu.roll(ext, s, 0)[8:, :] * cw[taps - 1 - s:taps - s, :]
    tail_scr[...] = x[tc - 8:, :]

    nblk = x.shape[1] // GATE_BLOCK
    zr, zi = [], []
    for kb in range(nblk):
        xb = xc[:, kb * GATE_BLOCK:(kb + 1) * GATE_BLOCK]
        zr.append(_bdot(xb, wr_ref[kb]))
        zi.append(_bdot(xb, wi_ref[kb]))
    r = jax.nn.sigmoid(jnp.concatenate(zr, axis=-1) + br_ref[...])
    ig = jax.nn.sigmoid(jnp.concatenate(zi, axis=-1) + bi_ref[...])
    lam = lam_ref[...]
    softplus_neg = jnp.maximum(-lam, 0.0) + jnp.log1p(jnp.exp(-jnp.abs(lam)))
    log_a = -RG_C * r * softplus_neg
    a = jnp.exp(log_a)
    u = jnp.sqrt(jnp.tanh(-log_a) * (1.0 + a * a)) * ig * xc
    a_cum, hz = _linear_scan_rows(a, u)
    h = hz + a_cum * h_scr[...]
    h_scr[...] = h[n_valid - 1:n_valid, :]
    o_ref[0] = (h * jax.nn.gelu(g_ref[0])).astype(o_ref.dtype)

    @pl.when(c == pl.num_programs(2) - 1)
    def _():
        hlast_ref[0] = h[n_valid - 1:n_valid, :]


def rglru(proj, conv_buf, h0, conv_w, conv_b, w_r, b_r, w_i, b_i, lam, *, width, n_valid, tc, bw=512):
    b, t, _ = proj.shape
    taps = conv_w.shape[0]
    assert t % tc == 0 and tc % 8 == 0 and (n_valid == tc or t == tc) and taps - 1 <= 8
    nw = width // bw
    gpb = bw // GATE_BLOCK
    tail0 = jnp.pad(conv_buf, ((0, 0), (8 - (taps - 1), 0), (0, 0)))
    vec = lambda a: a.reshape(1, width)
    vspec = pl.BlockSpec((1, bw), lambda bi, wi, ci: (0, wi))
    kern = functools.partial(_rglru_kernel, tc=tc, n_valid=n_valid)
    return pl.pallas_call(
        kern,
        grid=(b, nw, t // tc),
        in_specs=[pl.BlockSpec((1, tc, bw), lambda bi, wi, ci: (bi, ci, wi)),
                  pl.BlockSpec((1, tc, bw), lambda bi, wi, ci: (bi, ci, nw + wi)),
                  pl.BlockSpec((1, 8, bw), lambda bi, wi, ci: (bi, 0, wi)),
                  pl.BlockSpec((1, 1, bw), lambda bi, wi, ci: (bi, 0, wi)),
                  pl.BlockSpec((taps, bw), lambda bi, wi, ci: (0, wi)),
                  vspec,
                  pl.BlockSpec((gpb, GATE_BLOCK, GATE_BLOCK), lambda bi, wi, ci: (wi, 0, 0)),
                  vspec,
                  pl.BlockSpec((gpb, GATE_BLOCK, GATE_BLOCK), lambda bi, wi, ci: (wi, 0, 0)),
                  vspec, vspec],
        out_specs=[pl.BlockSpec((1, tc, bw), lambda bi, wi, ci: (bi, ci, wi)),
                   pl.BlockSpec((1, 1, bw), lambda bi, wi, ci: (bi, 0, wi))],
        out_shape=[jax.ShapeDtypeStruct((b, t, width), BF16),
                   jax.ShapeDtypeStruct((b, 1, width), F32)],
        scratch_shapes=[pltpu.VMEM((8, bw), F32), pltpu.VMEM((1, bw), F32)],
        compiler_params=_params(("parallel", "parallel", "arbitrary"), 32),
        name="rglru",
    )(proj, proj, tail0, h0.reshape(b, 1, width), conv_w, vec(conv_b), w_r, vec(b_r), w_i, vec(b_i), vec(lam))


def _hgrn_kernel(q_ref, f_ref, i_ref, g_ref, lbl_ref, ng_ref, s0_ref, o_ref, sout_ref, st_scr,
                 *, chunk, n_valid, layer):
    c = pl.program_id(2)

    @pl.when(c == 0)
    def _():
        st_scr[...] = s0_ref[0, 0].T

    lbl = lbl_ref[...]
    e = jnp.exp(lbl - jnp.max(lbl, axis=0, keepdims=True))
    soft = e / jnp.sum(e, axis=0, keepdims=True)
    lb = jnp.zeros((1, lbl.shape[1]), F32)
    for r in range(1, layer + 1):
        lb = lb + soft[r:r + 1, :]

    q = q_ref[0]
    fz = f_ref[0]
    v = i_ref[0]
    dk = q.shape[1]
    log_f = jnp.log(lb + (1.0 - lb) * jax.nn.sigmoid(fz))
    key = (1.0 - lb) * jax.nn.sigmoid(-fz)
    if n_valid < chunk:
        live = lax.broadcasted_iota(jnp.int32, (chunk, dk), 0) < n_valid
        log_f = jnp.where(live, log_f, 0.0)
        key = jnp.where(live, key, 0.0)
    cum = _cumsum_rows(log_f)
    last = cum[chunk - 1:chunk, :]
    st = st_scr[...]

    o = _bdot_nt(q * jnp.exp(cum), st)

    nsb = chunk // SUB
    parts = [jnp.zeros((SUB, v.shape[1]), F32)]
    for rb in range(1, nsb):
        lo = rb * SUB
        edge = cum[lo - 1:lo, :]
        qi = q[lo:lo + SUB] * jnp.exp(cum[lo:lo + SUB] - edge)
        kj = key[:lo] * jnp.exp(edge - cum[:lo])
        parts.append(_bdot(_bdot_nt(qi, kj), v[:lo]))
    o = o + jnp.concatenate(parts, axis=0)

    q3 = q.reshape(nsb, SUB, dk)
    k3 = key.reshape(nsb, SUB, dk)
    c3 = cum.reshape(nsb, SUB, dk)
    v3 = v.reshape(nsb, SUB, v.shape[1])
    ri = lax.broadcasted_iota(jnp.int32, (nsb, SUB, dk), 1)
    od = jnp.zeros(v3.shape, F32)
    for j in range(SUB):
        d = jnp.where(ri >= j, c3 - c3[:, j:j + 1, :], -jnp.inf)
        wgt = jnp.sum(q3 * jnp.exp(d) * k3[:, j:j + 1, :], axis=-1, keepdims=True)
        od = od + wgt * v3[:, j:j + 1, :]
    o = o + od.reshape(chunk, v.shape[1])

    st_new = st * jnp.exp(last) + _bdot_tn(v, key * jnp.exp(last - cum))
    st_scr[...] = st_new

    y = o * lax.rsqrt(jnp.mean(o * o, axis=-1, keepdims=True) + NORM_EPS) * ng_ref[...]
    o_ref[0] = (y * _silu(g_ref[0])).astype(o_ref.dtype)

    @pl.when(c == pl.num_programs(2) - 1)
    def _():
        sout_ref[0, 0] = st_new.T


def hgrn2(proj, lb_logits, norm_g, s0, *, layer, n_heads, dk, dv, col0, n_valid, chunk=128):
    b, t, _ = proj.shape
    assert dk == dv and t % chunk == 0 and chunk % SUB == 0 and (n_valid == chunk or t == chunk)
    h = n_heads
    kern = functools.partial(_hgrn_kernel, chunk=chunk, n_valid=n_valid, layer=layer)
    col = lambda off: pl.BlockSpec((1, chunk, dk), lambda bi, hi, ci, off=off: (bi, ci, col0 + off + hi))
    nl = lb_logits.shape[0]
    return pl.pallas_call(
        kern,
        grid=(b, h, t // chunk),
        in_specs=[col(0), col(h), col(2 * h), col(3 * h),
                  pl.BlockSpec((nl, dk), lambda bi, hi, ci: (0, hi)),
                  pl.BlockSpec((1, dv), lambda bi, hi, ci: (0, hi)),
                  pl.BlockSpec((1, 1, dk, dv), lambda bi, hi, ci: (bi, hi, 0, 0))],
        out_specs=[pl.BlockSpec((1, chunk, dv), lambda bi, hi, ci: (bi, ci, hi)),
                   pl.BlockSpec((1, 1, dk, dv), lambda bi, hi, ci: (bi, hi, 0, 0))],
        out_shape=[jax.ShapeDtypeStruct((b, t, h * dv), BF16),
                   jax.ShapeDtypeStruct((b, h, dk, dv), F32)],
        scratch_shapes=[pltpu.VMEM((dv, dk), F32)],
        compiler_params=_params(("parallel", "parallel", "arbitrary"), 32),
        name="hgrn2",
    )(proj, proj, proj, proj, lb_logits, norm_g.reshape(1, h * dv), s0)


def _rope_tables(pos, half):
    freq = ROPE_BASE ** (-jnp.arange(half, dtype=F32) / half)
    ang = pos.astype(F32)[:, None] * freq[None, :]
    return jnp.cos(ang), jnp.sin(ang)


def _pad_time(x, t_pad):
    return jnp.pad(x, ((0, 0), (0, t_pad - x.shape[1]), (0, 0)))


def kernel(x_prompt, x_sample, state_ret, cache_k, cache_v, state_rglru, state_conv, state_hgrn, page_table, norm_ffn1, norm_mix, norm_ffn2, ffn_gate, ffn_up, ffn_down, w_in_ab, w_out_ab, gn_ret, w_in_cd, w_out_cd, conv_w, conv_b, w_rgate, b_rgate, w_igate, b_igate, lru_lambda, hgrn_lb_logits, hgrn_norm, final_norm):
    depth = norm_ffn1.shape[0]
    d_model = x_prompt.shape[-1]
    n_ab, _, h_a, dk_a, dv_a = state_ret.shape
    h_b, dh_b = cache_k.shape[3], cache_k.shape[4]
    n_cd, _, w_c = state_rglru.shape
    _, _, h_d, dk_d, dv_d = state_hgrn.shape
    taps = conv_w.shape[1]
    past_len = page_table.shape[1] * cache_k.shape[2]
    a_cols = h_a * dk_a
    b_off = 4 * a_cols // dh_b

    wg = ffn_gate.astype(BF16)
    wu = ffn_up.astype(BF16)
    wd = ffn_down.astype(BF16)
    win_ab = w_in_ab.astype(BF16)
    wout_ab = w_out_ab.astype(BF16)
    win_cd = w_in_cd.astype(BF16)
    wout_cd = w_out_cd.astype(BF16)

    def ffn(x, l, j, g):
        xn = rmsnorm(x, g, BF16)
        return down_residual(gate_up(xn, wg[l, j], wu[l, j]), wd[l, j], x, 0.5)

    def forward(x3, pos0, ret_s, lru_h, conv_buf, hgrn_s, is_sample):
        nb, t, _ = x3.shape
        x = x3.reshape(nb * t, d_model)
        t_ret = t if not is_sample else 128
        t_lru = t if not is_sample else 8
        t_hg = t if not is_sample else 128
        ret_chunk = min(t_ret, 256)
        cos, sin = _rope_tables(pos0 + jnp.arange(t_ret), dk_a // 2)
        rets, ks, vs, hs, bufs, hgs = [], [], [], [], [], []
        for l in range(depth):
            i = l // 2
            x = ffn(x, l, 0, norm_ffn1[l])
            hn = rmsnorm(x, norm_mix[l], BF16)
            if l % 2 == 0:
                proj = matmul(hn, win_ab[i], F32).reshape(nb, t, -1)
                kb = proj[:, :, 4 * a_cols + h_b * dh_b:4 * a_cols + 2 * h_b * dh_b]
                vb = proj[:, :, 4 * a_cols + 2 * h_b * dh_b:]
                oa, s_new = retention(_pad_time(proj, t_ret), cos, sin, gn_ret[i], ret_s[i],
                                      n_heads=h_a, dk=dk_a, dv=dv_a, n_valid=min(t, ret_chunk), chunk=ret_chunk)
                oa = oa[:, :t]
                if is_sample:
                    qb = proj[:, :, 4 * a_cols:4 * a_cols + h_b * dh_b]
                    ob = moba_sample(qb, kb, vb, cache_k, cache_v, i, page_table, n_heads=h_b, dh=dh_b)
                else:
                    ob = moba_prompt(proj, n_heads=h_b, dh=dh_b, q_off=b_off, k_off=b_off + h_b, v_off=b_off + 2 * h_b)
                x = out_proj_residual(oa.reshape(nb * t, -1), ob.reshape(nb * t, -1).astype(BF16), wout_ab[i], x)
                rets.append(s_new)
                ks.append(kb.reshape(nb, t, h_b, dh_b))
                vs.append(vb.reshape(nb, t, h_b, dh_b))
            else:
                proj = matmul(hn, win_cd[i], F32).reshape(nb, t, -1)
                xin = jnp.concatenate([conv_buf[i], proj[:, :, :w_c]], axis=1)
                bufs.append(xin[:, t:])
                oc, h_last = rglru(_pad_time(proj, t_lru), conv_buf[i], lru_h[i], conv_w[i], conv_b[i],
                                   w_rgate[i], b_rgate[i], w_igate[i], b_igate[i], lru_lambda[i],
                                   width=w_c, n_valid=min(t, 256), tc=min(t_lru, 256))
                od, hg_new = hgrn2(_pad_time(proj, t_hg), hgrn_lb_logits, hgrn_norm[i], hgrn_s[i],
                                   layer=i, n_heads=h_d, dk=dk_d, dv=dv_d, col0=2 * w_c // dk_d,
                                   n_valid=min(t, 128), chunk=128)
                x = out_proj_residual(oc[:, :t].reshape(nb * t, -1), od[:, :t].reshape(nb * t, -1), wout_cd[i], x)
                hs.append(h_last.reshape(nb, w_c))
                hgs.append(hg_new)
            x = ffn(x, l, 1, norm_ffn2[l])
        y = rmsnorm(x, final_norm, F32).reshape(nb, t, d_model)
        return y, jnp.stack(rets), jnp.stack(ks), jnp.stack(vs), jnp.stack(hs), jnp.stack(bufs), jnp.stack(hgs)

    bp, tp = x_prompt.shape[0], x_prompt.shape[1]
    out_p = forward(
        x_prompt, 0,
        jnp.zeros((n_ab, bp, h_a, dk_a, dv_a), F32),
        jnp.zeros((n_cd, bp, w_c), F32),
        jnp.zeros((n_cd, bp, taps - 1, w_c), F32),
        jnp.zeros((n_cd, bp, h_d, dk_d, dv_d), F32),
        False)
    out_s = forward(x_sample, past_len, state_ret, state_rglru, state_conv, state_hgrn, True)
    return (out_p[0], out_s[0]) + tuple(out_p[1:]) + tuple(out_s[1:])
```

```python
import functools

import jax
import jax.numpy as jnp
from jax import lax
from jax.experimental import pallas as pl
from jax.experimental.pallas import tpu as pltpu

F32 = jnp.float32
BF16 = jnp.bfloat16

NORM_EPS = 1e-6
ROPE_BASE = 10000.0
MOBA_BLOCK = 256
MOBA_TOPK = 3
RG_C = 8.0
GATE_BLOCK = 128
SUB = 16
HGRN_HEADS_PER_STEP = 4
MIB = 1024 * 1024
ROW_BLOCK = 1024
COL_BLOCK = 512

_NT = (((1,), (1,)), ((), ()))
_TN = (((0,), (0,)), ((), ()))


def _params(semantics, vmem_mib):
    return pltpu.CompilerParams(dimension_semantics=semantics,
                                vmem_limit_bytes=int(vmem_mib * MIB))


def _bdot(a, b):
    return jnp.dot(a.astype(BF16), b.astype(BF16), preferred_element_type=F32)


def _bdot_nt(a, b):
    return lax.dot_general(a.astype(BF16), b.astype(BF16), _NT, preferred_element_type=F32)


def _bdot_tn(a, b):
    return lax.dot_general(a.astype(BF16), b.astype(BF16), _TN, preferred_element_type=F32)


def _silu(x):
    return x * jax.nn.sigmoid(x)


def _rms_kernel(x_ref, g_ref, o_ref):
    x = x_ref[...]
    ms = jnp.mean(x * x, axis=-1, keepdims=True)
    o_ref[...] = (x * lax.rsqrt(ms + NORM_EPS) * g_ref[...]).astype(o_ref.dtype)


def rmsnorm(x, g, out_dtype):
    m, d = x.shape
    bm = min(m, 256)
    return pl.pallas_call(
        _rms_kernel,
        grid=(pl.cdiv(m, bm),),
        in_specs=[pl.BlockSpec((bm, d), lambda i: (i, 0)),
                  pl.BlockSpec((1, d), lambda i: (0, 0))],
        out_specs=pl.BlockSpec((bm, d), lambda i: (i, 0)),
        out_shape=jax.ShapeDtypeStruct((m, d), out_dtype),
        compiler_params=_params(("parallel",), 32),
        name="rmsnorm",
    )(x, g.reshape(1, d))


def _row_split(body, bm, m):
    n_full, rem = divmod(m, bm)
    if rem == 0:
        body(bm)
        return
    i = pl.program_id(1)

    @pl.when(i < n_full)
    def _():
        body(bm)

    @pl.when(i == n_full)
    def _():
        body(rem)


def _mm_kernel(x_ref, w_ref, o_ref, *, bm, m):
    def body(rows):
        o_ref[:rows] = jnp.dot(x_ref[:rows], w_ref[...], preferred_element_type=F32).astype(o_ref.dtype)
    _row_split(body, bm, m)


def in_proj(x, w_all, idx, out_dtype):
    m, k = x.shape
    n = w_all.shape[-1]
    bm, bn = min(m, ROW_BLOCK), COL_BLOCK
    return pl.pallas_call(
        functools.partial(_mm_kernel, bm=bm, m=m),
        grid=(pl.cdiv(n, bn), pl.cdiv(m, bm)),
        in_specs=[pl.BlockSpec((bm, k), lambda j, i: (i, 0)),
                  pl.BlockSpec((None, k, bn), lambda j, i: (idx, 0, j))],
        out_specs=pl.BlockSpec((bm, bn), lambda j, i: (i, j)),
        out_shape=jax.ShapeDtypeStruct((m, n), out_dtype),
        compiler_params=_params(("parallel", "arbitrary"), 48),
        name="in_proj",
    )(x, w_all)


def _gate_up_kernel(x_ref, wg_ref, wu_ref, o_ref, *, bm, m):
    def body(rows):
        x = x_ref[:rows]
        g = jnp.dot(x, wg_ref[...], preferred_element_type=F32)
        u = jnp.dot(x, wu_ref[...], preferred_element_type=F32)
        o_ref[:rows] = (_silu(g) * u).astype(o_ref.dtype)
    _row_split(body, bm, m)


def gate_up(x, wg_all, wu_all, l, j2):
    m, k = x.shape
    n = wg_all.shape[-1]
    bm, bn = min(m, ROW_BLOCK), COL_BLOCK
    wspec = pl.BlockSpec((None, None, k, bn), lambda j, i: (l, j2, 0, j))
    return pl.pallas_call(
        functools.partial(_gate_up_kernel, bm=bm, m=m),
        grid=(pl.cdiv(n, bn), pl.cdiv(m, bm)),
        in_specs=[pl.BlockSpec((bm, k), lambda j, i: (i, 0)), wspec, wspec],
        out_specs=pl.BlockSpec((bm, bn), lambda j, i: (i, j)),
        out_shape=jax.ShapeDtypeStruct((m, n), BF16),
        compiler_params=_params(("parallel", "arbitrary"), 52),
        name="ffn_gate_up",
    )(x, wg_all, wu_all)


def _down_kernel(h_ref, w_ref, x_ref, o_ref, *, scale, bm, m):
    def body(rows):
        o_ref[:rows] = x_ref[:rows] + scale * jnp.dot(h_ref[:rows], w_ref[...], preferred_element_type=F32)
    _row_split(body, bm, m)


def down_residual(h, w_all, l, j2, x, scale):
    m, k = h.shape
    n = w_all.shape[-1]
    tk = k // 2
    assert tk * 2 == k and tk % 128 == 0
    bm, bn = min(m, ROW_BLOCK // 2), 2 * COL_BLOCK
    for kb in range(2):
        x = pl.pallas_call(
            functools.partial(_down_kernel, scale=scale, bm=bm, m=m),
            grid=(n // bn, pl.cdiv(m, bm)),
            in_specs=[pl.BlockSpec((bm, tk), lambda j, i, kb=kb: (i, kb)),
                      pl.BlockSpec((None, None, tk, bn), lambda j, i, kb=kb: (l, j2, kb, j)),
                      pl.BlockSpec((bm, bn), lambda j, i: (i, j))],
            out_specs=pl.BlockSpec((bm, bn), lambda j, i: (i, j)),
            out_shape=jax.ShapeDtypeStruct((m, n), F32),
            compiler_params=_params(("parallel", "arbitrary"), 52),
            name="ffn_down",
        )(h, w_all, x)
    return x


def _out_proj_kernel(a_ref, b_ref, wa_ref, wb_ref, x_ref, o_ref, *, bm, m):
    def body(rows):
        acc = jnp.dot(a_ref[:rows], wa_ref[...], preferred_element_type=F32)
        acc = acc + jnp.dot(b_ref[:rows], wb_ref[...], preferred_element_type=F32)
        o_ref[:rows] = x_ref[:rows] + acc
    _row_split(body, bm, m)


def out_proj_residual(a, b, w_all, idx, x):
    m, ka = a.shape
    assert b.shape[1] == ka and w_all.shape[1] == 2 * ka
    n = w_all.shape[-1]
    bm, bn = min(m, ROW_BLOCK), COL_BLOCK
    return pl.pallas_call(
        functools.partial(_out_proj_kernel, bm=bm, m=m),
        grid=(n // bn, pl.cdiv(m, bm)),
        in_specs=[pl.BlockSpec((bm, ka), lambda j, i: (i, 0)),
                  pl.BlockSpec((bm, ka), lambda j, i: (i, 0)),
                  pl.BlockSpec((None, ka, bn), lambda j, i: (idx, 0, j)),
                  pl.BlockSpec((None, ka, bn), lambda j, i: (idx, 1, j)),
                  pl.BlockSpec((bm, bn), lambda j, i: (i, j))],
        out_specs=pl.BlockSpec((bm, bn), lambda j, i: (i, j)),
        out_shape=jax.ShapeDtypeStruct((m, n), F32),
        compiler_params=_params(("parallel", "arbitrary"), 48),
        name="out_proj",
    )(a, b, w_all, w_all, x)


def _cumsum_rows(x):
    n = x.shape[0]
    row = lax.broadcasted_iota(jnp.int32, x.shape, 0)
    s = 1
    while s < n:
        x = x + jnp.where(row >= s, pltpu.roll(x, s, 0), 0.0)
        s *= 2
    return x


def _linear_scan_rows(a, u):
    n = a.shape[0]
    row = lax.broadcasted_iota(jnp.int32, a.shape, 0)
    s = 1
    while s < n:
        keep = row >= s
        a_prev = jnp.where(keep, pltpu.roll(a, s, 0), 1.0)
        u_prev = jnp.where(keep, pltpu.roll(u, s, 0), 0.0)
        u = a * u_prev + u
        a = a * a_prev
        s *= 2
    return a, u


def _retention_kernel(q_ref, k_ref, v_ref, g_ref, cos_ref, sin_ref, lg_ref, gn_ref, s0_ref,
                      o_ref, sout_ref, s_scr, dec_scr, *, chunk, n_valid, dk):
    c = pl.program_id(2)
    log_g = lg_ref[0][:, :1]

    @pl.when(c == 0)
    def _():
        s_scr[...] = s0_ref[0, 0]
        i = lax.broadcasted_iota(jnp.int32, (chunk, chunk), 0)
        j = lax.broadcasted_iota(jnp.int32, (chunk, chunk), 1)
        d = (i - j).astype(F32) * log_g
        dec_scr[...] = jnp.exp(jnp.where(i >= j, d, -jnp.inf))

    half = dk // 2
    cos = cos_ref[...]
    sin = sin_ref[...]

    def rot(x):
        x1 = x[:, :half]
        x2 = x[:, half:]
        return jnp.concatenate([x1 * cos - x2 * sin, x1 * sin + x2 * cos], axis=-1)

    q = rot(q_ref[...])
    k = rot(k_ref[...]) * (dk ** -0.5)
    v = v_ref[...]
    idx = lax.broadcasted_iota(jnp.int32, (chunk, 1), 0).astype(F32)
    inter = jnp.exp((idx + 1.0) * log_g)
    to_state = jnp.exp((n_valid - 1.0 - idx) * log_g)
    carry = jnp.exp(n_valid * log_g)

    s = s_scr[...]
    att = _bdot_nt(q, k) * dec_scr[...]
    o = _bdot(att, v) + _bdot(q, s) * inter
    s_new = carry * s + _bdot_tn(k * to_state, v)
    s_scr[...] = s_new

    mu = jnp.mean(o, axis=-1, keepdims=True)
    oc = o - mu
    var = jnp.mean(oc * oc, axis=-1, keepdims=True)
    y = oc * lax.rsqrt(var + NORM_EPS) * gn_ref[...]
    o_ref[...] = (y * _silu(g_ref[...])).astype(o_ref.dtype)

    @pl.when(c == pl.num_programs(2) - 1)
    def _():
        sout_ref[0, 0] = s_new


def retention(proj, row0, b, t, cos, sin, gn, s0, *, n_heads, dk, dv, n_valid, chunk):
    assert dk == dv and t % chunk == 0 and row0 % chunk == 0 and (n_valid == chunk or t == chunk)
    h = n_heads
    nc = t // chunk
    rb0 = row0 // chunk
    log_g = jnp.log1p(-jnp.exp2(-5.0 - jnp.arange(h, dtype=F32)))
    lg = jnp.broadcast_to(log_g[:, None, None], (h, 1, 128))
    kern = functools.partial(_retention_kernel, chunk=chunk, n_valid=n_valid, dk=dk)
    col = lambda off: pl.BlockSpec((chunk, dk), lambda bi, hi, ci, off=off: (rb0 + bi * nc + ci, off + hi))
    return pl.pallas_call(
        kern,
        grid=(b, h, nc),
        in_specs=[col(0), col(h), col(2 * h), col(3 * h),
                  pl.BlockSpec((chunk, dk // 2), lambda bi, hi, ci: (ci, 0)),
                  pl.BlockSpec((chunk, dk // 2), lambda bi, hi, ci: (ci, 0)),
                  pl.BlockSpec((1, 1, 128), lambda bi, hi, ci: (hi, 0, 0)),
                  pl.BlockSpec((1, dv), lambda bi, hi, ci: (0, hi)),
                  pl.BlockSpec((1, 1, dk, dv), lambda bi, hi, ci: (bi, hi, 0, 0))],
        out_specs=[pl.BlockSpec((chunk, dv), lambda bi, hi, ci: (bi * nc + ci, hi)),
                   pl.BlockSpec((1, 1, dk, dv), lambda bi, hi, ci: (bi, hi, 0, 0))],
        out_shape=[jax.ShapeDtypeStruct((b * t, h * dv), BF16),
                   jax.ShapeDtypeStruct((b, h, dk, dv), F32)],
        scratch_shapes=[pltpu.VMEM((dk, dv), F32), pltpu.VMEM((chunk, chunk), F32)],
        compiler_params=_params(("parallel", "parallel", "arbitrary"), 32),
        name="retention",
    )(proj, proj, proj, proj, cos, sin, lg, gn.reshape(1, h * dv), s0)


def _moba_prompt_kernel(q_ref, k_ref, v_ref, o_ref, kmean_scr, kb_scr, vb_scr, *, nb, scale):
    i = pl.program_id(2)
    blk = MOBA_BLOCK

    @pl.when(i == 0)
    def _():
        kk = k_ref[...]
        kmean_scr[...] = jnp.mean(kk.reshape(nb, blk, kk.shape[-1]), axis=1)
        kb_scr[...] = kk.astype(BF16)
        vb_scr[...] = v_ref[...].astype(BF16)

    q = q_ref[...]
    qb = q.astype(BF16)
    gate = lax.dot_general(q, kmean_scr[...], _NT, precision=lax.Precision.HIGHEST,
                           preferred_element_type=F32)
    col = lax.broadcasted_iota(jnp.int32, (blk, nb), 1)
    past = col < i
    gate = jnp.where(past, gate, -jnp.inf)
    rank = jnp.zeros(gate.shape, jnp.int32)
    for mm in range(nb):
        gm = gate[:, mm:mm + 1]
        rank = rank + ((gm > gate) | ((gm == gate) & (col > mm))).astype(jnp.int32)
    sel = jnp.where((rank < MOBA_TOPK) & past, 1.0, 0.0)

    row = lax.broadcasted_iota(jnp.int32, (blk, blk), 0)
    colk = lax.broadcasted_iota(jnp.int32, (blk, blk), 1)
    causal = colk <= row

    def attend(nkb):
        s = _bdot_nt(qb, kb_scr[:nkb * blk, :]) * scale
        pieces = []
        for n in range(nkb):
            own = jnp.full((blk, blk), n, jnp.int32) == i
            allowed = (jnp.broadcast_to(sel[:, n:n + 1], (blk, blk)) > 0.5) | (own & causal)
            pieces.append(jnp.where(allowed, s[:, n * blk:(n + 1) * blk], -jnp.inf))
        s = jnp.concatenate(pieces, axis=-1)
        m = jnp.max(s, axis=-1, keepdims=True)
        p = jnp.exp(s - m)
        l = jnp.sum(p, axis=-1, keepdims=True)
        o_ref[...] = (_bdot(p, vb_scr[:nkb * blk, :]) / l).astype(o_ref.dtype)

    step = 2 if nb % 2 == 0 else 1
    for nkb in range(step, nb + 1, step):
        @pl.when((i >= nkb - step) & (i < nkb))
        def _(nkb=nkb):
            attend(nkb)


def moba_prompt(proj, b, t, *, n_heads, dh, q_off, k_off, v_off):
    assert t % MOBA_BLOCK == 0
    nb = t // MOBA_BLOCK
    kern = functools.partial(_moba_prompt_kernel, nb=nb, scale=dh ** -0.5)
    return pl.pallas_call(
        kern,
        grid=(b, n_heads, nb),
        in_specs=[pl.BlockSpec((MOBA_BLOCK, dh), lambda bi, hi, ii: (bi * nb + ii, q_off + hi)),
                  pl.BlockSpec((t, dh), lambda bi, hi, ii: (bi, k_off + hi)),
                  pl.BlockSpec((t, dh), lambda bi, hi, ii: (bi, v_off + hi))],
        out_specs=pl.BlockSpec((MOBA_BLOCK, dh), lambda bi, hi, ii: (bi * nb + ii, hi)),
        out_shape=jax.ShapeDtypeStruct((b * t, n_heads * dh), BF16),
        scratch_shapes=[pltpu.VMEM((nb, dh), F32), pltpu.VMEM((t, dh), BF16), pltpu.VMEM((t, dh), BF16)],
        compiler_params=_params(("parallel", "parallel", "arbitrary"), 40),
        name="moba_prompt",
    )(proj, proj, proj)


def _moba_kmean_kernel(pt_ref, *refs):
    *k_refs, o_ref = refs
    acc = jnp.sum(k_refs[0][...], axis=0)
    for k_ref in k_refs[1:]:
        acc = acc + jnp.sum(k_ref[...], axis=0)
    o_ref[...] = acc * (1.0 / MOBA_BLOCK)


def _moba_decode_kernel(pt_ref, q_ref, kmean_ref, knew_ref, vnew_ref, *refs, n_heads, nblk, ts, ppb, scale):
    k_refs, v_refs = refs[:ppb], refs[ppb:2 * ppb]
    o_ref, sel_scr, bias_scr, m_scr, l_scr, acc_scr = refs[2 * ppb:]
    n = pl.program_id(1)
    q = q_ref[0]
    rows, dh = q.shape
    qb = q.astype(BF16)

    @pl.when(n == 0)
    def _():
        r = lax.broadcasted_iota(jnp.int32, bias_scr.shape, 0)
        c = lax.broadcasted_iota(jnp.int32, bias_scr.shape, 1)
        bias_scr[...] = jnp.where((r % n_heads) == (c % n_heads), 0.0, -jnp.inf)

        km = kmean_ref[0].reshape(nblk * n_heads, dh)
        g = lax.dot_general(q, km, _NT, precision=lax.Precision.HIGHEST, preferred_element_type=F32)
        width = nblk * n_heads
        nidx = lax.broadcasted_iota(jnp.int32, (rows, width), 1) // n_heads
        rank = jnp.zeros((rows, width), jnp.int32)
        for k in range(1, nblk):
            other = pltpu.roll(g, k * n_heads, 1)
            rank = rank + ((other > g) | ((other == g) & (nidx >= k))).astype(jnp.int32)
        sel_scr[...] = jnp.where(rank < MOBA_TOPK, 1.0, 0.0)

        kn = knew_ref[0].reshape(-1, dh)
        vn = vnew_ref[0].reshape(-1, dh)
        r = lax.broadcasted_iota(jnp.int32, (rows, kn.shape[0]), 0)
        c = lax.broadcasted_iota(jnp.int32, (rows, kn.shape[0]), 1)
        tj = c // n_heads
        ok = ((r % n_heads) == (c % n_heads)) & (tj <= r // n_heads) & (tj < ts)
        s = jnp.where(ok, _bdot_nt(qb, kn) * scale, -jnp.inf)
        m0 = jnp.max(s, axis=-1, keepdims=True)
        e = jnp.exp(s - m0)
        m_scr[...] = m0
        l_scr[...] = jnp.sum(e, axis=-1, keepdims=True)
        acc_scr[...] = _bdot(e, vn)

    rs = lax.broadcasted_iota(jnp.int32, sel_scr.shape, 0)
    cs = lax.broadcasted_iota(jnp.int32, sel_scr.shape, 1)
    flag = jnp.sum(jnp.where(cs == n * n_heads + rs % n_heads, sel_scr[...], 0.0), axis=-1, keepdims=True)
    bias = bias_scr[...] + jnp.where(flag > 0.5, 0.0, -jnp.inf)
    scores = [_bdot_nt(qb, k_ref[...].reshape(-1, dh)) * scale + bias for k_ref in k_refs]
    m_old = m_scr[...]
    m_new = m_old
    for s in scores:
        m_new = jnp.maximum(m_new, jnp.max(s, axis=-1, keepdims=True))
    alpha = jnp.exp(m_old - m_new)
    l = alpha * l_scr[...]
    acc = alpha * acc_scr[...]
    for s, v_ref in zip(scores, v_refs):
        e = jnp.exp(s - m_new)
        l = l + jnp.sum(e, axis=-1, keepdims=True)
        acc = acc + _bdot(e, v_ref[...].reshape(-1, dh))
    m_scr[...] = m_new
    l_scr[...] = l
    acc_scr[...] = acc

    @pl.when(n == pl.num_programs(1) - 1)
    def _():
        o_ref[0] = acc / l


def moba_sample(qb, knew, vnew, cache_k, cache_v, layer, page_table, *, n_heads, dh):
    b, ts, w = qb.shape
    n_pages = page_table.shape[1]
    page = cache_k.shape[2]
    past = n_pages * page
    assert past % MOBA_BLOCK == 0 and MOBA_BLOCK % page == 0 and ts <= MOBA_BLOCK and w == n_heads * dh
    ppb = MOBA_BLOCK // page
    nblk = past // MOBA_BLOCK
    rows = ts * n_heads
    ts_pad = -(-ts // 8) * 8
    pad = ((0, 0), (0, ts_pad - ts), (0, 0), (0, 0))
    q3 = qb.reshape(b, rows, dh)
    kn = jnp.pad(knew.reshape(b, ts, n_heads, dh), pad)
    vn = jnp.pad(vnew.reshape(b, ts, n_heads, dh), pad)
    page_specs = [pl.BlockSpec((None, None, page, n_heads, dh),
                               lambda bi, ni, pt, pg=pg: (layer, pt[bi, ni * ppb + pg], 0, 0, 0))
                  for pg in range(ppb)]

    kmean = pl.pallas_call(
        _moba_kmean_kernel,
        grid_spec=pltpu.PrefetchScalarGridSpec(
            num_scalar_prefetch=1,
            grid=(b, nblk),
            in_specs=page_specs,
            out_specs=pl.BlockSpec((None, None, n_heads, dh), lambda bi, ni, pt: (bi, ni, 0, 0))),
        out_shape=jax.ShapeDtypeStruct((b, nblk, n_heads, dh), F32),
        compiler_params=_params(("parallel", "arbitrary"), 32),
        name="moba_sample_kmean",
    )(page_table, *([cache_k] * ppb))

    out = pl.pallas_call(
        functools.partial(_moba_decode_kernel, n_heads=n_heads, nblk=nblk, ts=ts, ppb=ppb, scale=dh ** -0.5),
        grid_spec=pltpu.PrefetchScalarGridSpec(
            num_scalar_prefetch=1,
            grid=(b, nblk),
            in_specs=[pl.BlockSpec((1, rows, dh), lambda bi, ni, pt: (bi, 0, 0)),
                      pl.BlockSpec((1, nblk, n_heads, dh), lambda bi, ni, pt: (bi, 0, 0, 0)),
                      pl.BlockSpec((1, ts_pad, n_heads, dh), lambda bi, ni, pt: (bi, 0, 0, 0)),
                      pl.BlockSpec((1, ts_pad, n_heads, dh), lambda bi, ni, pt: (bi, 0, 0, 0))]
                     + page_specs + page_specs,
            out_specs=pl.BlockSpec((1, rows, dh), lambda bi, ni, pt: (bi, 0, 0)),
            scratch_shapes=[pltpu.VMEM((rows, nblk * n_heads), F32), pltpu.VMEM((rows, page * n_heads), F32),
                            pltpu.VMEM((rows, 1), F32), pltpu.VMEM((rows, 1), F32), pltpu.VMEM((rows, dh), F32)]),
        out_shape=jax.ShapeDtypeStruct((b, rows, dh), F32),
        compiler_params=_params(("parallel", "arbitrary"), 40),
        name="moba_sample_attn",
    )(page_table, q3, kmean, kn, vn, *([cache_k] * ppb), *([cache_v] * ppb))
    return out.reshape(b, ts, w)


def _rglru_kernel(x_ref, g_ref, tail0_ref, h0_ref, cw_ref, cb_ref, wr_ref, br_ref, wi_ref, bi_ref, lam_ref,
                  o_ref, hlast_ref, tail_scr, h_scr, *, tc, n_valid):
    c = pl.program_id(2)

    @pl.when(c == 0)
    def _():
        tail_scr[...] = tail0_ref[0]
        h_scr[...] = h0_ref[0]

    x = x_ref[...]
    cw = cw_ref[...]
    taps = cw.shape[0]
    ext = jnp.concatenate([tail_scr[...], x], axis=0)
    xc = cb_ref[...] + x * cw[taps - 1:taps, :]
    for s in range(1, taps):
        xc = xc + pltpu.roll(ext, s, 0)[8:, :] * cw[taps - 1 - s:taps - s, :]
    tail_scr[...] = x[tc - 8:, :]

    nblk = x.shape[1] // GATE_BLOCK
    zr, zi = [], []
    for kb in range(nblk):
        xb = xc[:, kb * GATE_BLOCK:(kb + 1) * GATE_BLOCK]
        zr.append(_bdot(xb, wr_ref[kb]))
        zi.append(_bdot(xb, wi_ref[kb]))
    r = jax.nn.sigmoid(jnp.concatenate(zr, axis=-1) + br_ref[...])
    ig = jax.nn.sigmoid(jnp.concatenate(zi, axis=-1) + bi_ref[...])
    lam = lam_ref[...]
    softplus_neg = jnp.maximum(-lam, 0.0) + jnp.log1p(jnp.exp(-jnp.abs(lam)))
    log_a = -RG_C * r * softplus_neg
    a = jnp.exp(log_a)
    u = jnp.sqrt(jnp.tanh(-log_a) * (1.0 + a * a)) * ig * xc
    a_cum, hz = _linear_scan_rows(a, u)
    h = hz + a_cum * h_scr[...]
    h_scr[...] = h[n_valid - 1:n_valid, :]
    o_ref[...] = (h * jax.nn.gelu(g_ref[...])).astype(o_ref.dtype)

    @pl.when(c == pl.num_programs(2) - 1)
    def _():
        hlast_ref[0] = h[n_valid - 1:n_valid, :]


def rglru(proj, row0, b, t, conv_buf, h0, conv_w, conv_b, w_r, b_r, w_i, b_i, lam, *, width, n_valid, tc, bw=512):
    taps = conv_w.shape[0]
    assert t % tc == 0 and tc % 8 == 0 and row0 % tc == 0 and (n_valid == tc or t == tc) and taps - 1 <= 8
    nw = width // bw
    gpb = bw // GATE_BLOCK
    nc = t // tc
    rb0 = row0 // tc
    tail0 = jnp.pad(conv_buf, ((0, 0), (8 - (taps - 1), 0), (0, 0)))
    vec = lambda a: a.reshape(1, width)
    vspec = pl.BlockSpec((1, bw), lambda bi, wi, ci: (0, wi))
    kern = functools.partial(_rglru_kernel, tc=tc, n_valid=n_valid)
    return pl.pallas_call(
        kern,
        grid=(b, nw, nc),
        in_specs=[pl.BlockSpec((tc, bw), lambda bi, wi, ci: (rb0 + bi * nc + ci, wi)),
                  pl.BlockSpec((tc, bw), lambda bi, wi, ci: (rb0 + bi * nc + ci, nw + wi)),
                  pl.BlockSpec((1, 8, bw), lambda bi, wi, ci: (bi, 0, wi)),
                  pl.BlockSpec((1, 1, bw), lambda bi, wi, ci: (bi, 0, wi)),
                  pl.BlockSpec((taps, bw), lambda bi, wi, ci: (0, wi)),
                  vspec,
                  pl.BlockSpec((gpb, GATE_BLOCK, GATE_BLOCK), lambda bi, wi, ci: (wi, 0, 0)),
                  vspec,
                  pl.BlockSpec((gpb, GATE_BLOCK, GATE_BLOCK), lambda bi, wi, ci: (wi, 0, 0)),
                  vspec, vspec],
        out_specs=[pl.BlockSpec((tc, bw), lambda bi, wi, ci: (bi * nc + ci, wi)),
                   pl.BlockSpec((1, 1, bw), lambda bi, wi, ci: (bi, 0, wi))],
        out_shape=[jax.ShapeDtypeStruct((b * t, width), BF16),
                   jax.ShapeDtypeStruct((b, 1, width), F32)],
        scratch_shapes=[pltpu.VMEM((8, bw), F32), pltpu.VMEM((1, bw), F32)],
        compiler_params=_params(("parallel", "parallel", "arbitrary"), 32),
        name="rglru",
    )(proj, proj, tail0, h0.reshape(b, 1, width), conv_w, vec(conv_b), w_r, vec(b_r), w_i, vec(b_i), vec(lam))


def _hgrn_head(q, fz, v, g, lb, ng, st, *, chunk, n_valid):
    dk = q.shape[1]
    log_f = jnp.log(lb + (1.0 - lb) * jax.nn.sigmoid(fz))
    key = (1.0 - lb) * jax.nn.sigmoid(-fz)
    if n_valid < chunk:
        live = lax.broadcasted_iota(jnp.int32, (chunk, dk), 0) < n_valid
        log_f = jnp.where(live, log_f, 0.0)
        key = jnp.where(live, key, 0.0)
    cum = _cumsum_rows(log_f)
    last = cum[chunk - 1:chunk, :]

    o = _bdot_nt(q * jnp.exp(cum), st)

    row = lax.broadcasted_iota(jnp.int32, (chunk, dk), 0)
    ri = lax.broadcasted_iota(jnp.int32, (chunk, chunk), 0)
    ci = lax.broadcasted_iota(jnp.int32, (chunk, chunk), 1)
    att = None
    s = chunk // 2
    while s >= SUB:
        grp = chunk // (2 * s)
        edge = jnp.broadcast_to(cum.reshape(grp, 2 * s, dk)[:, s - 1:s, :], (grp, 2 * s, dk)).reshape(chunk, dk)
        upper = (row % (2 * s)) >= s
        qs = q * jnp.exp(jnp.where(upper, cum - edge, -jnp.inf))
        ks = key * jnp.exp(jnp.where(upper, -jnp.inf, edge - cum))
        a = _bdot_nt(qs, ks)
        if grp > 1:
            a = jnp.where((ri // (2 * s)) == (ci // (2 * s)), a, 0.0)
        att = a if att is None else att + a
        s //= 2
    if att is not None:
        o = o + _bdot(att, v)

    nsb = chunk // SUB
    q3 = q.reshape(nsb, SUB, dk)
    k3 = key.reshape(nsb, SUB, dk)
    c3 = cum.reshape(nsb, SUB, dk)
    v3 = v.reshape(nsb, SUB, v.shape[1])
    ri = lax.broadcasted_iota(jnp.int32, (nsb, SUB, dk), 1)
    od = jnp.zeros(v3.shape, F32)
    for j in range(SUB):
        d = jnp.where(ri >= j, c3 - c3[:, j:j + 1, :], -jnp.inf)
        wgt = jnp.sum(q3 * jnp.exp(d) * k3[:, j:j + 1, :], axis=-1, keepdims=True)
        od = od + wgt * v3[:, j:j + 1, :]
    o = o + od.reshape(chunk, v.shape[1])

    st_new = st * jnp.exp(last) + _bdot_tn(v, key * jnp.exp(last - cum))
    y = o * lax.rsqrt(jnp.mean(o * o, axis=-1, keepdims=True) + NORM_EPS) * ng
    return y * _silu(g), st_new


def _hgrn_kernel(q_ref, f_ref, i_ref, g_ref, lbl_ref, ng_ref, s0_ref, o_ref, sout_ref, st_scr,
                 *, chunk, n_valid, layer, hp, dk):
    c = pl.program_id(2)

    @pl.when(c == 0)
    def _():
        for k in range(hp):
            st_scr[k] = s0_ref[0, k].T

    lbl = lbl_ref[...]
    e = jnp.exp(lbl - jnp.max(lbl, axis=0, keepdims=True))
    soft = e / jnp.sum(e, axis=0, keepdims=True)
    lb_all = jnp.zeros((1, lbl.shape[1]), F32)
    for r in range(1, layer + 1):
        lb_all = lb_all + soft[r:r + 1, :]

    outs, states = [], []
    for k in range(hp):
        sl = slice(k * dk, (k + 1) * dk)
        y, st_new = _hgrn_head(q_ref[:, sl], f_ref[:, sl], i_ref[:, sl], g_ref[:, sl], lb_all[:, sl],
                               ng_ref[:, sl], st_scr[k], chunk=chunk, n_valid=n_valid)
        st_scr[k] = st_new
        outs.append(y)
        states.append(st_new)
    o_ref[...] = jnp.concatenate(outs, axis=-1).astype(o_ref.dtype)

    @pl.when(c == pl.num_programs(2) - 1)
    def _():
        for k in range(hp):
            sout_ref[0, k] = states[k].T


def hgrn2(proj, row0, b, t, lb_logits, norm_g, s0, *, layer, n_heads, dk, dv, col0, n_valid, chunk=128):
    hp = HGRN_HEADS_PER_STEP
    assert dk == dv and t % chunk == 0 and chunk % SUB == 0 and row0 % chunk == 0 and (n_valid == chunk or t == chunk)
    assert n_heads % hp == 0 and col0 % hp == 0
    h = n_heads
    nc = t // chunk
    rb0 = row0 // chunk
    kern = functools.partial(_hgrn_kernel, chunk=chunk, n_valid=n_valid, layer=layer, hp=hp, dk=dk)
    col = lambda off: pl.BlockSpec((chunk, hp * dk),
                                   lambda bi, hi, ci, off=off: (rb0 + bi * nc + ci, (col0 + off) // hp + hi))
    nl = lb_logits.shape[0]
    return pl.pallas_call(
        kern,
        grid=(b, h // hp, nc),
        in_specs=[col(0), col(h), col(2 * h), col(3 * h),
                  pl.BlockSpec((nl, hp * dk), lambda bi, hi, ci: (0, hi)),
                  pl.BlockSpec((1, hp * dv), lambda bi, hi, ci: (0, hi)),
                  pl.BlockSpec((1, hp, dk, dv), lambda bi, hi, ci: (bi, hi, 0, 0))],
        out_specs=[pl.BlockSpec((chunk, hp * dv), lambda bi, hi, ci: (bi * nc + ci, hi)),
                   pl.BlockSpec((1, hp, dk, dv), lambda bi, hi, ci: (bi, hi, 0, 0))],
        out_shape=[jax.ShapeDtypeStruct((b * t, h * dv), BF16),
                   jax.ShapeDtypeStruct((b, h, dk, dv), F32)],
        scratch_shapes=[pltpu.VMEM((hp, dv, dk), F32)],
        compiler_params=_params(("parallel", "parallel", "arbitrary"), 32),
        name="hgrn2",
    )(proj, proj, proj, proj, lb_logits, norm_g.reshape(1, h * dv), s0)


def _rope_tables(pos, half):
    freq = ROPE_BASE ** (-jnp.arange(half, dtype=F32) / half)
    ang = pos.astype(F32)[:, None] * freq[None, :]
    return jnp.cos(ang), jnp.sin(ang)


def _pad_seq(x, b, t, t_pad):
    c = x.shape[1]
    return jnp.pad(x.reshape(b, t, c), ((0, 0), (0, t_pad - t), (0, 0))).reshape(b * t_pad, c)


def _unpad_seq(x, b, t, t_pad):
    return x.reshape(b, t_pad, x.shape[1])[:, :t].reshape(b * t, x.shape[1])


def kernel(x_prompt, x_sample, state_ret, cache_k, cache_v, state_rglru, state_conv, state_hgrn, page_table, norm_ffn1, norm_mix, norm_ffn2, ffn_gate, ffn_up, ffn_down, w_in_ab, w_out_ab, gn_ret, w_in_cd, w_out_cd, conv_w, conv_b, w_rgate, b_rgate, w_igate, b_igate, lru_lambda, hgrn_lb_logits, hgrn_norm, final_norm):
    depth = norm_ffn1.shape[0]
    d_model = x_prompt.shape[-1]
    bp, tp = x_prompt.shape[0], x_prompt.shape[1]
    bs, ts = x_sample.shape[0], x_sample.shape[1]
    mp, ms = bp * tp, bs * ts
    n_ab, _, h_a, dk_a, dv_a = state_ret.shape
    h_b, dh_b = cache_k.shape[3], cache_k.shape[4]
    n_cd, _, w_c = state_rglru.shape
    _, _, h_d, dk_d, dv_d = state_hgrn.shape
    taps = conv_w.shape[1]
    past_len = page_table.shape[1] * cache_k.shape[2]
    a_cols = h_a * dk_a
    b_cols = h_b * dh_b
    b_off = 4 * a_cols // dh_b

    wg = ffn_gate.astype(BF16)
    wu = ffn_up.astype(BF16)
    wd = ffn_down.astype(BF16)
    win_ab = w_in_ab.astype(BF16)
    wout_ab = w_out_ab.astype(BF16)
    win_cd = w_in_cd.astype(BF16)
    wout_cd = w_out_cd.astype(BF16)

    ts_ret, ts_lru, ts_hg = 128, 8, 128
    ret_chunk_p = min(tp, 256)
    cos_p, sin_p = _rope_tables(jnp.arange(tp), dk_a // 2)
    cos_s, sin_s = _rope_tables(past_len + jnp.arange(ts_ret), dk_a // 2)
    zeros = lambda *s: jnp.zeros(s, F32)

    def ffn(x, l, j2, g):
        xn = rmsnorm(x, g, BF16)
        return down_residual(gate_up(xn, wg, wu, l, j2), wd, l, j2, x, 0.5)

    x = jnp.concatenate([x_prompt.reshape(mp, d_model), x_sample.reshape(ms, d_model)], axis=0)
    ret_p, ret_s, k_p, k_s, v_p, v_s = [], [], [], [], [], []
    h_p, h_s, buf_p, buf_s, hg_p, hg_s = [], [], [], [], [], []
    for l in range(depth):
        i = l // 2
        x = ffn(x, l, 0, norm_ffn1[l])
        hn = rmsnorm(x, norm_mix[l], BF16)
        if l % 2 == 0:
            proj = in_proj(hn, win_ab, i, F32)
            proj_s = proj[mp:]
            kb_cols = slice(4 * a_cols + b_cols, 4 * a_cols + 2 * b_cols)
            vb_cols = slice(4 * a_cols + 2 * b_cols, 4 * a_cols + 3 * b_cols)
            k_p.append(proj[:mp, kb_cols].reshape(bp, tp, h_b, dh_b))
            v_p.append(proj[:mp, vb_cols].reshape(bp, tp, h_b, dh_b))
            k_s.append(proj_s[:, kb_cols].reshape(bs, ts, h_b, dh_b))
            v_s.append(proj_s[:, vb_cols].reshape(bs, ts, h_b, dh_b))

            oa_p, s_p = retention(proj, 0, bp, tp, cos_p, sin_p, gn_ret[i], zeros(bp, h_a, dk_a, dv_a),
                                  n_heads=h_a, dk=dk_a, dv=dv_a, n_valid=ret_chunk_p, chunk=ret_chunk_p)
            oa_s, s_s = retention(_pad_seq(proj_s[:, :4 * a_cols], bs, ts, ts_ret), 0, bs, ts_ret, cos_s, sin_s,
                                  gn_ret[i], state_ret[i], n_heads=h_a, dk=dk_a, dv=dv_a, n_valid=ts, chunk=ts_ret)
            ob_p = moba_prompt(proj, bp, tp, n_heads=h_b, dh=dh_b, q_off=b_off, k_off=b_off + h_b, v_off=b_off + 2 * h_b)
            ob_s = moba_sample(proj_s[:, 4 * a_cols:4 * a_cols + b_cols].reshape(bs, ts, b_cols),
                               proj_s[:, kb_cols].reshape(bs, ts, b_cols), proj_s[:, vb_cols].reshape(bs, ts, b_cols),
                               cache_k, cache_v, i, page_table, n_heads=h_b, dh=dh_b)
            oa = jnp.concatenate([oa_p, _unpad_seq(oa_s, bs, ts, ts_ret)], axis=0)
            ob = jnp.concatenate([ob_p, ob_s.reshape(ms, b_cols).astype(BF16)], axis=0)
            x = out_proj_residual(oa, ob, wout_ab, i, x)
            ret_p.append(s_p)
            ret_s.append(s_s)
        else:
            proj = in_proj(hn, win_cd, i, F32)
            proj_s = proj[mp:]
            xb_p = proj[:mp, :w_c].reshape(bp, tp, w_c)
            xb_s = proj_s[:, :w_c].reshape(bs, ts, w_c)
            buf_p.append(jnp.concatenate([zeros(bp, taps - 1, w_c), xb_p], axis=1)[:, tp:])
            buf_s.append(jnp.concatenate([state_conv[i], xb_s], axis=1)[:, ts:])

            lru = functools.partial(rglru, conv_w=conv_w[i], conv_b=conv_b[i], w_r=w_rgate[i], b_r=b_rgate[i],
                                    w_i=w_igate[i], b_i=b_igate[i], lam=lru_lambda[i], width=w_c)
            oc_p, hl_p = lru(proj, 0, bp, tp, zeros(bp, taps - 1, w_c), zeros(bp, w_c), n_valid=min(tp, 256), tc=min(tp, 256))
            oc_s, hl_s = lru(_pad_seq(proj_s[:, :2 * w_c], bs, ts, ts_lru), 0, bs, ts_lru, state_conv[i], state_rglru[i],
                             n_valid=ts, tc=ts_lru)
            hg = functools.partial(hgrn2, lb_logits=hgrn_lb_logits, norm_g=hgrn_norm[i], layer=i, n_heads=h_d,
                                   dk=dk_d, dv=dv_d, col0=2 * w_c // dk_d, chunk=128)
            od_p, sg_p = hg(proj, 0, bp, tp, s0=zeros(bp, h_d, dk_d, dv_d), n_valid=128)
            od_s, sg_s = hg(_pad_seq(proj_s, bs, ts, ts_hg), 0, bs, ts_hg, s0=state_hgrn[i], n_valid=ts)
            oc = jnp.concatenate([oc_p, _unpad_seq(oc_s, bs, ts, ts_lru)], axis=0)
            od = jnp.concatenate([od_p, _unpad_seq(od_s, bs, ts, ts_hg)], axis=0)
            x = out_proj_residual(oc, od, wout_cd, i, x)
            h_p.append(hl_p.reshape(bp, w_c))
            h_s.append(hl_s.reshape(bs, w_c))
            hg_p.append(sg_p)
            hg_s.append(sg_s)
        x = ffn(x, l, 1, norm_ffn2[l])
    y = rmsnorm(x, final_norm, F32)
    st = jnp.stack
    return (y[:mp].reshape(bp, tp, d_model), y[mp:].reshape(bs, ts, d_model),
            st(ret_p), st(k_p), st(v_p), st(h_p), st(buf_p), st(hg_p),
            st(ret_s), st(k_s), st(v_s), st(h_s), st(buf_s), st(hg_s))
```

```python
import functools

import jax
import jax.numpy as jnp
from jax import lax
from jax.experimental import pallas as pl
from jax.experimental.pallas import tpu as pltpu

F32 = jnp.float32
BF16 = jnp.bfloat16

NORM_EPS = 1e-6
ROPE_BASE = 10000.0
MOBA_BLOCK = 256
MOBA_TOPK = 3
RG_C = 8.0
GATE_BLOCK = 128
SUB = 16
HGRN_HEADS_PER_STEP = 4
MIB = 1024 * 1024
ROW_BLOCK = 1024
COL_BLOCK = 512

_NT = (((1,), (1,)), ((), ()))
_TN = (((0,), (0,)), ((), ()))


def _params(semantics, vmem_mib):
    return pltpu.CompilerParams(dimension_semantics=semantics,
                                vmem_limit_bytes=int(vmem_mib * MIB))


def _bdot(a, b):
    return jnp.dot(a.astype(BF16), b.astype(BF16), preferred_element_type=F32)


def _bdot_nt(a, b):
    return lax.dot_general(a.astype(BF16), b.astype(BF16), _NT, preferred_element_type=F32)


def _bdot_tn(a, b):
    return lax.dot_general(a.astype(BF16), b.astype(BF16), _TN, preferred_element_type=F32)


def _silu(x):
    return x * jax.nn.sigmoid(x)


def _rms_kernel(x_ref, g_ref, o_ref):
    x = x_ref[...]
    ms = jnp.mean(x * x, axis=-1, keepdims=True)
    o_ref[...] = (x * lax.rsqrt(ms + NORM_EPS) * g_ref[...]).astype(o_ref.dtype)


def rmsnorm(x, g, out_dtype):
    m, d = x.shape
    bm = min(m, 256)
    return pl.pallas_call(
        _rms_kernel,
        grid=(pl.cdiv(m, bm),),
        in_specs=[pl.BlockSpec((bm, d), lambda i: (i, 0)),
                  pl.BlockSpec((1, d), lambda i: (0, 0))],
        out_specs=pl.BlockSpec((bm, d), lambda i: (i, 0)),
        out_shape=jax.ShapeDtypeStruct((m, d), out_dtype),
        compiler_params=_params(("parallel",), 32),
        name="rmsnorm",
    )(x, g.reshape(1, d))


def _row_order(m, bm):
    nb = pl.cdiv(m, bm)
    if m % bm == 0:
        return nb, (lambda i: i)
    return nb, (lambda i: (i + nb - 1) % nb)


def _row_split(body, bm, m):
    rem = m % bm
    if rem == 0:
        body(bm)
        return
    i = pl.program_id(1)

    @pl.when(i == 0)
    def _():
        body(rem)

    @pl.when(i > 0)
    def _():
        body(bm)


def _round_weights(pairs):
    @pl.when(pl.program_id(1) == 0)
    def _():
        for src, dst in pairs:
            dst[...] = src[...].astype(BF16)


def _mm_kernel(x_ref, w_ref, o_ref, wb, *, bm, m):
    _round_weights([(w_ref, wb)])

    def body(rows):
        o_ref[:rows] = jnp.dot(x_ref[:rows], wb[...], preferred_element_type=F32).astype(o_ref.dtype)
    _row_split(body, bm, m)


def in_proj(x, w_all, idx, out_dtype):
    m, k = x.shape
    n = w_all.shape[-1]
    bm, bn = min(m, ROW_BLOCK), COL_BLOCK
    steps, rb = _row_order(m, bm)
    return pl.pallas_call(
        functools.partial(_mm_kernel, bm=bm, m=m),
        grid=(pl.cdiv(n, bn), steps),
        in_specs=[pl.BlockSpec((bm, k), lambda j, i: (rb(i), 0)),
                  pl.BlockSpec((None, k, bn), lambda j, i: (idx, 0, j))],
        out_specs=pl.BlockSpec((bm, bn), lambda j, i: (rb(i), j)),
        out_shape=jax.ShapeDtypeStruct((m, n), out_dtype),
        scratch_shapes=[pltpu.VMEM((k, bn), BF16)],
        compiler_params=_params(("parallel", "arbitrary"), 52),
        name="in_proj",
    )(x, w_all)


def _gate_up_kernel(x_ref, wg_ref, wu_ref, o_ref, wgb, wub, *, bm, m):
    _round_weights([(wg_ref, wgb), (wu_ref, wub)])

    def body(rows):
        x = x_ref[:rows]
        g = jnp.dot(x, wgb[...], preferred_element_type=F32)
        u = jnp.dot(x, wub[...], preferred_element_type=F32)
        o_ref[:rows] = (_silu(g) * u).astype(o_ref.dtype)
    _row_split(body, bm, m)


def gate_up(x, wg_all, wu_all, l, j2):
    m, k = x.shape
    n = wg_all.shape[-1]
    bm, bn = min(m, ROW_BLOCK), COL_BLOCK // 2
    steps, rb = _row_order(m, bm)
    wspec = pl.BlockSpec((None, None, k, bn), lambda j, i: (l, j2, 0, j))
    return pl.pallas_call(
        functools.partial(_gate_up_kernel, bm=bm, m=m),
        grid=(pl.cdiv(n, bn), steps),
        in_specs=[pl.BlockSpec((bm, k), lambda j, i: (rb(i), 0)), wspec, wspec],
        out_specs=pl.BlockSpec((bm, bn), lambda j, i: (rb(i), j)),
        out_shape=jax.ShapeDtypeStruct((m, n), BF16),
        scratch_shapes=[pltpu.VMEM((k, bn), BF16), pltpu.VMEM((k, bn), BF16)],
        compiler_params=_params(("parallel", "arbitrary"), 52),
        name="ffn_gate_up",
    )(x, wg_all, wu_all)


def _down_kernel(h_ref, w_ref, x_ref, o_ref, wb, *, scale, bm, m):
    _round_weights([(w_ref, wb)])

    def body(rows):
        o_ref[:rows] = x_ref[:rows] + scale * jnp.dot(h_ref[:rows], wb[...], preferred_element_type=F32)
    _row_split(body, bm, m)


def down_residual(h, w_all, l, j2, x, scale):
    m, k = h.shape
    n = w_all.shape[-1]
    tk = k // 2
    assert tk * 2 == k and tk % 128 == 0
    bm, bn = min(m, ROW_BLOCK // 2), COL_BLOCK
    steps, rb = _row_order(m, bm)
    for kb in range(2):
        x = pl.pallas_call(
            functools.partial(_down_kernel, scale=scale, bm=bm, m=m),
            grid=(n // bn, steps),
            in_specs=[pl.BlockSpec((bm, tk), lambda j, i, kb=kb: (rb(i), kb)),
                      pl.BlockSpec((None, None, tk, bn), lambda j, i, kb=kb: (l, j2, kb, j)),
                      pl.BlockSpec((bm, bn), lambda j, i: (rb(i), j))],
            out_specs=pl.BlockSpec((bm, bn), lambda j, i: (rb(i), j)),
            out_shape=jax.ShapeDtypeStruct((m, n), F32),
            scratch_shapes=[pltpu.VMEM((tk, bn), BF16)],
            compiler_params=_params(("parallel", "arbitrary"), 54),
            name="ffn_down",
        )(h, w_all, x)
    return x


def _out_proj_kernel(a_ref, b_ref, wa_ref, wb_ref, x_ref, o_ref, wab, wbb, *, bm, m):
    _round_weights([(wa_ref, wab), (wb_ref, wbb)])

    def body(rows):
        acc = jnp.dot(a_ref[:rows], wab[...], preferred_element_type=F32)
        acc = acc + jnp.dot(b_ref[:rows], wbb[...], preferred_element_type=F32)
        o_ref[:rows] = x_ref[:rows] + acc
    _row_split(body, bm, m)


def out_proj_residual(a, b, w_all, idx, x):
    m, ka = a.shape
    assert b.shape[1] == ka and w_all.shape[1] == 2 * ka
    n = w_all.shape[-1]
    bm, bn = min(m, ROW_BLOCK), COL_BLOCK
    steps, rb = _row_order(m, bm)
    return pl.pallas_call(
        functools.partial(_out_proj_kernel, bm=bm, m=m),
        grid=(n // bn, steps),
        in_specs=[pl.BlockSpec((bm, ka), lambda j, i: (rb(i), 0)),
                  pl.BlockSpec((bm, ka), lambda j, i: (rb(i), 0)),
                  pl.BlockSpec((None, ka, bn), lambda j, i: (idx, 0, j)),
                  pl.BlockSpec((None, ka, bn), lambda j, i: (idx, 1, j)),
                  pl.BlockSpec((bm, bn), lambda j, i: (rb(i), j))],
        out_specs=pl.BlockSpec((bm, bn), lambda j, i: (rb(i), j)),
        out_shape=jax.ShapeDtypeStruct((m, n), F32),
        scratch_shapes=[pltpu.VMEM((ka, bn), BF16), pltpu.VMEM((ka, bn), BF16)],
        compiler_params=_params(("parallel", "arbitrary"), 54),
        name="out_proj",
    )(a, b, w_all, w_all, x)


def _cumsum_rows(x):
    n = x.shape[0]
    row = lax.broadcasted_iota(jnp.int32, x.shape, 0)
    s = 1
    while s < n:
        x = x + jnp.where(row >= s, pltpu.roll(x, s, 0), 0.0)
        s *= 2
    return x


def _linear_scan_rows(a, u):
    n = a.shape[0]
    row = lax.broadcasted_iota(jnp.int32, a.shape, 0)
    s = 1
    while s < n:
        keep = row >= s
        a_prev = jnp.where(keep, pltpu.roll(a, s, 0), 1.0)
        u_prev = jnp.where(keep, pltpu.roll(u, s, 0), 0.0)
        u = a * u_prev + u
        a = a * a_prev
        s *= 2
    return a, u


def _retention_kernel(q_ref, k_ref, v_ref, g_ref, cos_ref, sin_ref, lg_ref, gn_ref, s0_ref,
                      o_ref, sout_ref, s_scr, dec_scr, *, chunk, n_valid, dk):
    c = pl.program_id(2)
    log_g = lg_ref[0][:, :1]

    @pl.when(c == 0)
    def _():
        s_scr[...] = s0_ref[0, 0]
        i = lax.broadcasted_iota(jnp.int32, (chunk, chunk), 0)
        j = lax.broadcasted_iota(jnp.int32, (chunk, chunk), 1)
        d = (i - j).astype(F32) * log_g
        dec_scr[...] = jnp.exp(jnp.where(i >= j, d, -jnp.inf))

    half = dk // 2
    cos = cos_ref[...]
    sin = sin_ref[...]

    def rot(x):
        x1 = x[:, :half]
        x2 = x[:, half:]
        return jnp.concatenate([x1 * cos - x2 * sin, x1 * sin + x2 * cos], axis=-1)

    q = rot(q_ref[...])
    k = rot(k_ref[...]) * (dk ** -0.5)
    v = v_ref[...]
    idx = lax.broadcasted_iota(jnp.int32, (chunk, 1), 0).astype(F32)
    inter = jnp.exp((idx + 1.0) * log_g)
    to_state = jnp.exp((n_valid - 1.0 - idx) * log_g)
    carry = jnp.exp(n_valid * log_g)

    s = s_scr[...]
    att = _bdot_nt(q, k) * dec_scr[...]
    o = _bdot(att, v) + _bdot(q, s) * inter
    s_new = carry * s + _bdot_tn(k * to_state, v)
    s_scr[...] = s_new

    mu = jnp.mean(o, axis=-1, keepdims=True)
    oc = o - mu
    var = jnp.mean(oc * oc, axis=-1, keepdims=True)
    y = oc * lax.rsqrt(var + NORM_EPS) * gn_ref[...]
    o_ref[...] = (y * _silu(g_ref[...])).astype(o_ref.dtype)

    @pl.when(c == pl.num_programs(2) - 1)
    def _():
        sout_ref[0, 0] = s_new


def retention(proj, row0, b, t, cos, sin, gn, s0, *, n_heads, dk, dv, n_valid, chunk):
    assert dk == dv and t % chunk == 0 and row0 % chunk == 0 and (n_valid == chunk or t == chunk)
    h = n_heads
    nc = t // chunk
    rb0 = row0 // chunk
    log_g = jnp.log1p(-jnp.exp2(-5.0 - jnp.arange(h, dtype=F32)))
    lg = jnp.broadcast_to(log_g[:, None, None], (h, 1, 128))
    kern = functools.partial(_retention_kernel, chunk=chunk, n_valid=n_valid, dk=dk)
    col = lambda off: pl.BlockSpec((chunk, dk), lambda bi, hi, ci, off=off: (rb0 + bi * nc + ci, off + hi))
    return pl.pallas_call(
        kern,
        grid=(b, h, nc),
        in_specs=[col(0), col(h), col(2 * h), col(3 * h),
                  pl.BlockSpec((chunk, dk // 2), lambda bi, hi, ci: (ci, 0)),
                  pl.BlockSpec((chunk, dk // 2), lambda bi, hi, ci: (ci, 0)),
                  pl.BlockSpec((1, 1, 128), lambda bi, hi, ci: (hi, 0, 0)),
                  pl.BlockSpec((1, dv), lambda bi, hi, ci: (0, hi)),
                  pl.BlockSpec((1, 1, dk, dv), lambda bi, hi, ci: (bi, hi, 0, 0))],
        out_specs=[pl.BlockSpec((chunk, dv), lambda bi, hi, ci: (bi * nc + ci, hi)),
                   pl.BlockSpec((1, 1, dk, dv), lambda bi, hi, ci: (bi, hi, 0, 0))],
        out_shape=[jax.ShapeDtypeStruct((b * t, h * dv), BF16),
                   jax.ShapeDtypeStruct((b, h, dk, dv), F32)],
        scratch_shapes=[pltpu.VMEM((dk, dv), F32), pltpu.VMEM((chunk, chunk), F32)],
        compiler_params=_params(("parallel", "parallel", "arbitrary"), 32),
        name="retention",
    )(proj, proj, proj, proj, cos, sin, lg, gn.reshape(1, h * dv), s0)


def _moba_prompt_kernel(q_ref, k_ref, v_ref, o_ref, kmean_scr, kb_scr, vb_scr, *, nb, scale):
    i = pl.program_id(2)
    blk = MOBA_BLOCK

    @pl.when(i == 0)
    def _():
        kk = k_ref[...]
        kmean_scr[...] = jnp.mean(kk.reshape(nb, blk, kk.shape[-1]), axis=1)
        kb_scr[...] = kk.astype(BF16)
        vb_scr[...] = v_ref[...].astype(BF16)

    q = q_ref[...]
    qb = q.astype(BF16)
    gate = lax.dot_general(kmean_scr[...], q, _NT, precision=lax.Precision.HIGHEST,
                           preferred_element_type=F32)
    bidx = lax.broadcasted_iota(jnp.int32, (nb, blk), 0)
    past = bidx < i
    gate = jnp.where(past, gate, -jnp.inf)
    rank = jnp.zeros(gate.shape, jnp.int32)
    for mm in range(nb):
        gm = gate[mm:mm + 1, :]
        rank = rank + ((gm > gate) | ((gm == gate) & (bidx > mm))).astype(jnp.int32)
    sel = jnp.where((rank < MOBA_TOPK) & past, 1.0, 0.0).T

    row = lax.broadcasted_iota(jnp.int32, (blk, blk), 0)
    colk = lax.broadcasted_iota(jnp.int32, (blk, blk), 1)
    causal = colk <= row

    def attend(nkb):
        s = _bdot_nt(qb, kb_scr[:nkb * blk, :]) * scale
        pieces = []
        for n in range(nkb):
            own = jnp.full((blk, blk), n, jnp.int32) == i
            allowed = (jnp.broadcast_to(sel[:, n:n + 1], (blk, blk)) > 0.5) | (own & causal)
            pieces.append(jnp.where(allowed, s[:, n * blk:(n + 1) * blk], -jnp.inf))
        s = jnp.concatenate(pieces, axis=-1)
        m = jnp.max(s, axis=-1, keepdims=True)
        p = jnp.exp(s - m)
        l = jnp.sum(p, axis=-1, keepdims=True)
        o_ref[...] = (_bdot(p, vb_scr[:nkb * blk, :]) / l).astype(o_ref.dtype)

    step = 2 if nb % 2 == 0 else 1
    for nkb in range(step, nb + 1, step):
        @pl.when((i >= nkb - step) & (i < nkb))
        def _(nkb=nkb):
            attend(nkb)


def moba_prompt(proj, b, t, *, n_heads, dh, q_off, k_off, v_off):
    assert t % MOBA_BLOCK == 0
    nb = t // MOBA_BLOCK
    kern = functools.partial(_moba_prompt_kernel, nb=nb, scale=dh ** -0.5)
    return pl.pallas_call(
        kern,
        grid=(b, n_heads, nb),
        in_specs=[pl.BlockSpec((MOBA_BLOCK, dh), lambda bi, hi, ii: (bi * nb + ii, q_off + hi)),
                  pl.BlockSpec((t, dh), lambda bi, hi, ii: (bi, k_off + hi)),
                  pl.BlockSpec((t, dh), lambda bi, hi, ii: (bi, v_off + hi))],
        out_specs=pl.BlockSpec((MOBA_BLOCK, dh), lambda bi, hi, ii: (bi * nb + ii, hi)),
        out_shape=jax.ShapeDtypeStruct((b * t, n_heads * dh), BF16),
        scratch_shapes=[pltpu.VMEM((nb, dh), F32), pltpu.VMEM((t, dh), BF16), pltpu.VMEM((t, dh), BF16)],
        compiler_params=_params(("parallel", "parallel", "arbitrary"), 40),
        name="moba_prompt",
    )(proj, proj, proj)


def _moba_kmean_kernel(pt_ref, *refs):
    *k_refs, o_ref = refs
    acc = jnp.sum(k_refs[0][...], axis=0)
    for k_ref in k_refs[1:]:
        acc = acc + jnp.sum(k_ref[...], axis=0)
    o_ref[...] = acc * (1.0 / MOBA_BLOCK)


def _moba_decode_kernel(pt_ref, q_ref, kmean_ref, knew_ref, vnew_ref, *refs, n_heads, nblk, ts, ppb, scale):
    k_refs, v_refs = refs[:ppb], refs[ppb:2 * ppb]
    o_ref, sel_scr, bias_scr, m_scr, l_scr, acc_scr = refs[2 * ppb:]
    n = pl.program_id(1)
    q = q_ref[0]
    rows, dh = q.shape
    qb = q.astype(BF16)

    @pl.when(n == 0)
    def _():
        r = lax.broadcasted_iota(jnp.int32, bias_scr.shape, 0)
        c = lax.broadcasted_iota(jnp.int32, bias_scr.shape, 1)
        bias_scr[...] = jnp.where((r % n_heads) == (c % n_heads), 0.0, -jnp.inf)

        km = kmean_ref[0].reshape(nblk * n_heads, dh)
        g = lax.dot_general(q, km, _NT, precision=lax.Precision.HIGHEST, preferred_element_type=F32)
        width = nblk * n_heads
        nidx = lax.broadcasted_iota(jnp.int32, (rows, width), 1) // n_heads
        rank = jnp.zeros((rows, width), jnp.int32)
        for k in range(1, nblk):
            other = pltpu.roll(g, k * n_heads, 1)
            rank = rank + ((other > g) | ((other == g) & (nidx >= k))).astype(jnp.int32)
        sel_scr[...] = jnp.where(rank < MOBA_TOPK, 1.0, 0.0)

        kn = knew_ref[0].reshape(-1, dh)
        vn = vnew_ref[0].reshape(-1, dh)
        r = lax.broadcasted_iota(jnp.int32, (rows, kn.shape[0]), 0)
        c = lax.broadcasted_iota(jnp.int32, (rows, kn.shape[0]), 1)
        tj = c // n_heads
        ok = ((r % n_heads) == (c % n_heads)) & (tj <= r // n_heads) & (tj < ts)
        s = jnp.where(ok, _bdot_nt(qb, kn) * scale, -jnp.inf)
        m0 = jnp.max(s, axis=-1, keepdims=True)
        e = jnp.exp(s - m0)
        m_scr[...] = m0
        l_scr[...] = jnp.sum(e, axis=-1, keepdims=True)
        acc_scr[...] = _bdot(e, vn)

    rs = lax.broadcasted_iota(jnp.int32, sel_scr.shape, 0)
    cs = lax.broadcasted_iota(jnp.int32, sel_scr.shape, 1)
    flag = jnp.sum(jnp.where(cs == n * n_heads + rs % n_heads, sel_scr[...], 0.0), axis=-1, keepdims=True)
    bias = bias_scr[...] + jnp.where(flag > 0.5, 0.0, -jnp.inf)
    scores = [_bdot_nt(qb, k_ref[...].reshape(-1, dh)) * scale + bias for k_ref in k_refs]
    m_old = m_scr[...]
    m_new = m_old
    for s in scores:
        m_new = jnp.maximum(m_new, jnp.max(s, axis=-1, keepdims=True))
    alpha = jnp.exp(m_old - m_new)
    l = alpha * l_scr[...]
    acc = alpha * acc_scr[...]
    for s, v_ref in zip(scores, v_refs):
        e = jnp.exp(s - m_new)
        l = l + jnp.sum(e, axis=-1, keepdims=True)
        acc = acc + _bdot(e, v_ref[...].reshape(-1, dh))
    m_scr[...] = m_new
    l_scr[...] = l
    acc_scr[...] = acc

    @pl.when(n == pl.num_programs(1) - 1)
    def _():
        o_ref[0] = acc / l


def moba_sample(qb, knew, vnew, cache_k, cache_v, layer, page_table, *, n_heads, dh):
    b, ts, w = qb.shape
    n_pages = page_table.shape[1]
    page = cache_k.shape[2]
    past = n_pages * page
    assert past % MOBA_BLOCK == 0 and MOBA_BLOCK % page == 0 and ts <= MOBA_BLOCK and w == n_heads * dh
    ppb = MOBA_BLOCK // page
    nblk = past // MOBA_BLOCK
    rows = ts * n_heads
    ts_pad = -(-ts // 8) * 8
    pad = ((0, 0), (0, ts_pad - ts), (0, 0), (0, 0))
    q3 = qb.reshape(b, rows, dh)
    kn = jnp.pad(knew.reshape(b, ts, n_heads, dh), pad)
    vn = jnp.pad(vnew.reshape(b, ts, n_heads, dh), pad)
    page_specs = [pl.BlockSpec((None, None, page, n_heads, dh),
                               lambda bi, ni, pt, pg=pg: (layer, pt[bi, ni * ppb + pg], 0, 0, 0))
                  for pg in range(ppb)]

    kmean = pl.pallas_call(
        _moba_kmean_kernel,
        grid_spec=pltpu.PrefetchScalarGridSpec(
            num_scalar_prefetch=1,
            grid=(b, nblk),
            in_specs=page_specs,
            out_specs=pl.BlockSpec((None, None, n_heads, dh), lambda bi, ni, pt: (bi, ni, 0, 0))),
        out_shape=jax.ShapeDtypeStruct((b, nblk, n_heads, dh), F32),
        compiler_params=_params(("parallel", "arbitrary"), 32),
        name="moba_sample_kmean",
    )(page_table, *([cache_k] * ppb))

    out = pl.pallas_call(
        functools.partial(_moba_decode_kernel, n_heads=n_heads, nblk=nblk, ts=ts, ppb=ppb, scale=dh ** -0.5),
        grid_spec=pltpu.PrefetchScalarGridSpec(
            num_scalar_prefetch=1,
            grid=(b, nblk),
            in_specs=[pl.BlockSpec((1, rows, dh), lambda bi, ni, pt: (bi, 0, 0)),
                      pl.BlockSpec((1, nblk, n_heads, dh), lambda bi, ni, pt: (bi, 0, 0, 0)),
                      pl.BlockSpec((1, ts_pad, n_heads, dh), lambda bi, ni, pt: (bi, 0, 0, 0)),
                      pl.BlockSpec((1, ts_pad, n_heads, dh), lambda bi, ni, pt: (bi, 0, 0, 0))]
                     + page_specs + page_specs,
            out_specs=pl.BlockSpec((1, rows, dh), lambda bi, ni, pt: (bi, 0, 0)),
            scratch_shapes=[pltpu.VMEM((rows, nblk * n_heads), F32), pltpu.VMEM((rows, page * n_heads), F32),
                            pltpu.VMEM((rows, 1), F32), pltpu.VMEM((rows, 1), F32), pltpu.VMEM((rows, dh), F32)]),
        out_shape=jax.ShapeDtypeStruct((b, rows, dh), F32),
        compiler_params=_params(("parallel", "arbitrary"), 40),
        name="moba_sample_attn",
    )(page_table, q3, kmean, kn, vn, *([cache_k] * ppb), *([cache_v] * ppb))
    return out.reshape(b, ts, w)


def _rglru_kernel(x_ref, g_ref, tail0_ref, h0_ref, cw_ref, cb_ref, wr_ref, br_ref, wi_ref, bi_ref, lam_ref,
                  o_ref, hlast_ref, tail_scr, h_scr, *, tc, n_valid):
    c = pl.program_id(2)

    @pl.when(c == 0)
    def _():
        tail_scr[...] = tail0_ref[0]
        h_scr[...] = h0_ref[0]

    x = x_ref[...]
    cw = cw_ref[...]
    taps = cw.shape[0]
    ext = jnp.concatenate([tail_scr[...], x], axis=0)
    xc = cb_ref[...] + x * cw[taps - 1:taps, :]
    for s in range(1, taps):
        xc = xc + pltpu.roll(ext, s, 0)[8:, :] * cw[taps - 1 - s:taps - s, :]
    tail_scr[...] = x[tc - 8:, :]

    nblk = x.shape[1] // GATE_BLOCK
    zr, zi = [], []
    for kb in range(nblk):
        xb = xc[:, kb * GATE_BLOCK:(kb + 1) * GATE_BLOCK]
        zr.append(_bdot(xb, wr_ref[kb]))
        zi.append(_bdot(xb, wi_ref[kb]))
    r = jax.nn.sigmoid(jnp.concatenate(zr, axis=-1) + br_ref[...])
    ig = jax.nn.sigmoid(jnp.concatenate(zi, axis=-1) + bi_ref[...])
    lam = lam_ref[...]
    softplus_neg = jnp.maximum(-lam, 0.0) + jnp.log1p(jnp.exp(-jnp.abs(lam)))
    log_a = -RG_C * r * softplus_neg
    a = jnp.exp(log_a)
    u = jnp.sqrt(jnp.tanh(-log_a) * (1.0 + a * a)) * ig * xc
    a_cum, hz = _linear_scan_rows(a, u)
    h = hz + a_cum * h_scr[...]
    h_scr[...] = h[n_valid - 1:n_valid, :]
    o_ref[...] = (h * jax.nn.gelu(g_ref[...])).astype(o_ref.dtype)

    @pl.when(c == pl.num_programs(2) - 1)
    def _():
        hlast_ref[0] = h[n_valid - 1:n_valid, :]


def rglru(proj, row0, b, t, conv_buf, h0, conv_w, conv_b, w_r, b_r, w_i, b_i, lam, *, width, n_valid, tc, bw=512):
    taps = conv_w.shape[0]
    assert t % tc == 0 and tc % 8 == 0 and row0 % tc == 0 and (n_valid == tc or t == tc) and taps - 1 <= 8
    nw = width // bw
    gpb = bw // GATE_BLOCK
    nc = t // tc
    rb0 = row0 // tc
    tail0 = jnp.pad(conv_buf, ((0, 0), (8 - (taps - 1), 0), (0, 0)))
    vec = lambda a: a.reshape(1, width)
    vspec = pl.BlockSpec((1, bw), lambda bi, wi, ci: (0, wi))
    kern = functools.partial(_rglru_kernel, tc=tc, n_valid=n_valid)
    return pl.pallas_call(
        kern,
        grid=(b, nw, nc),
        in_specs=[pl.BlockSpec((tc, bw), lambda bi, wi, ci: (rb0 + bi * nc + ci, wi)),
                  pl.BlockSpec((tc, bw), lambda bi, wi, ci: (rb0 + bi * nc + ci, nw + wi)),
                  pl.BlockSpec((1, 8, bw), lambda bi, wi, ci: (bi, 0, wi)),
                  pl.BlockSpec((1, 1, bw), lambda bi, wi, ci: (bi, 0, wi)),
                  pl.BlockSpec((taps, bw), lambda bi, wi, ci: (0, wi)),
                  vspec,
                  pl.BlockSpec((gpb, GATE_BLOCK, GATE_BLOCK), lambda bi, wi, ci: (wi, 0, 0)),
                  vspec,
                  pl.BlockSpec((gpb, GATE_BLOCK, GATE_BLOCK), lambda bi, wi, ci: (wi, 0, 0)),
                  vspec, vspec],
        out_specs=[pl.BlockSpec((tc, bw), lambda bi, wi, ci: (bi * nc + ci, wi)),
                   pl.BlockSpec((1, 1, bw), lambda bi, wi, ci: (bi, 0, wi))],
        out_shape=[jax.ShapeDtypeStruct((b * t, width), BF16),
                   jax.ShapeDtypeStruct((b, 1, width), F32)],
        scratch_shapes=[pltpu.VMEM((8, bw), F32), pltpu.VMEM((1, bw), F32)],
        compiler_params=_params(("parallel", "parallel", "arbitrary"), 32),
        name="rglru",
    )(proj, proj, tail0, h0.reshape(b, 1, width), conv_w, vec(conv_b), w_r, vec(b_r), w_i, vec(b_i), vec(lam))


def _hgrn_head(q, fz, v, g, lb, ng, st, *, chunk, n_valid):
    dk = q.shape[1]
    log_f = jnp.log(lb + (1.0 - lb) * jax.nn.sigmoid(fz))
    key = (1.0 - lb) * jax.nn.sigmoid(-fz)
    if n_valid < chunk:
        live = lax.broadcasted_iota(jnp.int32, (chunk, dk), 0) < n_valid
        log_f = jnp.where(live, log_f, 0.0)
        key = jnp.where(live, key, 0.0)
    cum = _cumsum_rows(log_f)
    last = cum[chunk - 1:chunk, :]

    o = _bdot_nt(q * jnp.exp(cum), st)

    row = lax.broadcasted_iota(jnp.int32, (chunk, dk), 0)
    ri = lax.broadcasted_iota(jnp.int32, (chunk, chunk), 0)
    ci = lax.broadcasted_iota(jnp.int32, (chunk, chunk), 1)
    att = None
    s = chunk // 2
    while s >= SUB:
        grp = chunk // (2 * s)
        edge = jnp.broadcast_to(cum.reshape(grp, 2 * s, dk)[:, s - 1:s, :], (grp, 2 * s, dk)).reshape(chunk, dk)
        upper = (row % (2 * s)) >= s
        qs = q * jnp.exp(jnp.where(upper, cum - edge, -jnp.inf))
        ks = key * jnp.exp(jnp.where(upper, -jnp.inf, edge - cum))
        a = _bdot_nt(qs, ks)
        if grp > 1:
            a = jnp.where((ri // (2 * s)) == (ci // (2 * s)), a, 0.0)
        att = a if att is None else att + a
        s //= 2
    if att is not None:
        o = o + _bdot(att, v)

    nsb = chunk // SUB
    q3 = q.reshape(nsb, SUB, dk)
    k3 = key.reshape(nsb, SUB, dk)
    c3 = cum.reshape(nsb, SUB, dk)
    v3 = v.reshape(nsb, SUB, v.shape[1])
    ri = lax.broadcasted_iota(jnp.int32, (nsb, SUB, dk), 1)
    od = jnp.zeros(v3.shape, F32)
    for j in range(SUB):
        d = jnp.where(ri >= j, c3 - c3[:, j:j + 1, :], -jnp.inf)
        wgt = jnp.sum(q3 * jnp.exp(d) * k3[:, j:j + 1, :], axis=-1, keepdims=True)
        od = od + wgt * v3[:, j:j + 1, :]
    o = o + od.reshape(chunk, v.shape[1])

    st_new = st * jnp.exp(last) + _bdot_tn(v, key * jnp.exp(last - cum))
    y = o * lax.rsqrt(jnp.mean(o * o, axis=-1, keepdims=True) + NORM_EPS) * ng
    return y * _silu(g), st_new


def _hgrn_kernel(q_ref, f_ref, i_ref, g_ref, lbl_ref, ng_ref, s0_ref, o_ref, sout_ref, st_scr,
                 *, chunk, n_valid, layer, hp, dk):
    c = pl.program_id(2)

    @pl.when(c == 0)
    def _():
        for k in range(hp):
            st_scr[k] = s0_ref[0, k].T

    lbl = lbl_ref[...]
    e = jnp.exp(lbl - jnp.max(lbl, axis=0, keepdims=True))
    soft = e / jnp.sum(e, axis=0, keepdims=True)
    lb_all = jnp.zeros((1, lbl.shape[1]), F32)
    for r in range(1, layer + 1):
        lb_all = lb_all + soft[r:r + 1, :]

    outs, states = [], []
    for k in range(hp):
        sl = slice(k * dk, (k + 1) * dk)
        y, st_new = _hgrn_head(q_ref[:, sl], f_ref[:, sl], i_ref[:, sl], g_ref[:, sl], lb_all[:, sl],
                               ng_ref[:, sl], st_scr[k], chunk=chunk, n_valid=n_valid)
        st_scr[k] = st_new
        outs.append(y)
        states.append(st_new)
    o_ref[...] = jnp.concatenate(outs, axis=-1).astype(o_ref.dtype)

    @pl.when(c == pl.num_programs(2) - 1)
    def _():
        for k in range(hp):
            sout_ref[0, k] = states[k].T


def hgrn2(proj, row0, b, t, lb_logits, norm_g, s0, *, layer, n_heads, dk, dv, col0, n_valid, chunk=128):
    hp = HGRN_HEADS_PER_STEP
    assert dk == dv and t % chunk == 0 and chunk % SUB == 0 and row0 % chunk == 0 and (n_valid == chunk or t == chunk)
    assert n_heads % hp == 0 and col0 % hp == 0
    h = n_heads
    nc = t // chunk
    rb0 = row0 // chunk
    kern = functools.partial(_hgrn_kernel, chunk=chunk, n_valid=n_valid, layer=layer, hp=hp, dk=dk)
    col = lambda off: pl.BlockSpec((chunk, hp * dk),
                                   lambda bi, hi, ci, off=off: (rb0 + bi * nc + ci, (col0 + off) // hp + hi))
    nl = lb_logits.shape[0]
    return pl.pallas_call(
        kern,
        grid=(b, h // hp, nc),
        in_specs=[col(0), col(h), col(2 * h), col(3 * h),
                  pl.BlockSpec((nl, hp * dk), lambda bi, hi, ci: (0, hi)),
                  pl.BlockSpec((1, hp * dv), lambda bi, hi, ci: (0, hi)),
                  pl.BlockSpec((1, hp, dk, dv), lambda bi, hi, ci: (bi, hi, 0, 0))],
        out_specs=[pl.BlockSpec((chunk, hp * dv), lambda bi, hi, ci: (bi * nc + ci, hi)),
                   pl.BlockSpec((1, hp, dk, dv), lambda bi, hi, ci: (bi, hi, 0, 0))],
        out_shape=[jax.ShapeDtypeStruct((b * t, h * dv), BF16),
                   jax.ShapeDtypeStruct((b, h, dk, dv), F32)],
        scratch_shapes=[pltpu.VMEM((hp, dv, dk), F32)],
        compiler_params=_params(("parallel", "parallel", "arbitrary"), 32),
        name="hgrn2",
    )(proj, proj, proj, proj, lb_logits, norm_g.reshape(1, h * dv), s0)


def _rope_tables(pos, half):
    freq = ROPE_BASE ** (-jnp.arange(half, dtype=F32) / half)
    ang = pos.astype(F32)[:, None] * freq[None, :]
    return jnp.cos(ang), jnp.sin(ang)


def _pad_seq(x, b, t, t_pad):
    c = x.shape[1]
    return jnp.pad(x.reshape(b, t, c), ((0, 0), (0, t_pad - t), (0, 0))).reshape(b * t_pad, c)


def _unpad_seq(x, b, t, t_pad):
    return x.reshape(b, t_pad, x.shape[1])[:, :t].reshape(b * t, x.shape[1])


def kernel(x_prompt, x_sample, state_ret, cache_k, cache_v, state_rglru, state_conv, state_hgrn, page_table, norm_ffn1, norm_mix, norm_ffn2, ffn_gate, ffn_up, ffn_down, w_in_ab, w_out_ab, gn_ret, w_in_cd, w_out_cd, conv_w, conv_b, w_rgate, b_rgate, w_igate, b_igate, lru_lambda, hgrn_lb_logits, hgrn_norm, final_norm):
    depth = norm_ffn1.shape[0]
    d_model = x_prompt.shape[-1]
    bp, tp = x_prompt.shape[0], x_prompt.shape[1]
    bs, ts = x_sample.shape[0], x_sample.shape[1]
    mp, ms = bp * tp, bs * ts
    n_ab, _, h_a, dk_a, dv_a = state_ret.shape
    h_b, dh_b = cache_k.shape[3], cache_k.shape[4]
    n_cd, _, w_c = state_rglru.shape
    _, _, h_d, dk_d, dv_d = state_hgrn.shape
    taps = conv_w.shape[1]
    past_len = page_table.shape[1] * cache_k.shape[2]
    a_cols = h_a * dk_a
    b_cols = h_b * dh_b
    b_off = 4 * a_cols // dh_b

    wg, wu, wd = ffn_gate, ffn_up, ffn_down
    win_ab, wout_ab, win_cd, wout_cd = w_in_ab, w_out_ab, w_in_cd, w_out_cd

    ts_ret, ts_lru, ts_hg = 128, 8, 128
    ret_chunk_p = min(tp, 256)
    cos_p, sin_p = _rope_tables(jnp.arange(tp), dk_a // 2)
    cos_s, sin_s = _rope_tables(past_len + jnp.arange(ts_ret), dk_a // 2)
    zeros = lambda *s: jnp.zeros(s, F32)

    def ffn(x, l, j2, g):
        xn = rmsnorm(x, g, BF16)
        return down_residual(gate_up(xn, wg, wu, l, j2), wd, l, j2, x, 0.5)

    x = jnp.concatenate([x_prompt.reshape(mp, d_model), x_sample.reshape(ms, d_model)], axis=0)
    ret_p, ret_s, k_p, k_s, v_p, v_s = [], [], [], [], [], []
    h_p, h_s, buf_p, buf_s, hg_p, hg_s = [], [], [], [], [], []
    for l in range(depth):
        i = l // 2
        x = ffn(x, l, 0, norm_ffn1[l])
        hn = rmsnorm(x, norm_mix[l], BF16)
        if l % 2 == 0:
            proj = in_proj(hn, win_ab, i, F32)
            proj_s = proj[mp:]
            kb_cols = slice(4 * a_cols + b_cols, 4 * a_cols + 2 * b_cols)
            vb_cols = slice(4 * a_cols + 2 * b_cols, 4 * a_cols + 3 * b_cols)
            k_p.append(proj[:mp, kb_cols].reshape(bp, tp, h_b, dh_b))
            v_p.append(proj[:mp, vb_cols].reshape(bp, tp, h_b, dh_b))
            k_s.append(proj_s[:, kb_cols].reshape(bs, ts, h_b, dh_b))
            v_s.append(proj_s[:, vb_cols].reshape(bs, ts, h_b, dh_b))

            oa_p, s_p = retention(proj, 0, bp, tp, cos_p, sin_p, gn_ret[i], zeros(bp, h_a, dk_a, dv_a),
                                  n_heads=h_a, dk=dk_a, dv=dv_a, n_valid=ret_chunk_p, chunk=ret_chunk_p)
            oa_s, s_s = retention(_pad_seq(proj_s[:, :4 * a_cols], bs, ts, ts_ret), 0, bs, ts_ret, cos_s, sin_s,
                                  gn_ret[i], state_ret[i], n_heads=h_a, dk=dk_a, dv=dv_a, n_valid=ts, chunk=ts_ret)
            ob_p = moba_prompt(proj, bp, tp, n_heads=h_b, dh=dh_b, q_off=b_off, k_off=b_off + h_b, v_off=b_off + 2 * h_b)
            ob_s = moba_sample(proj_s[:, 4 * a_cols:4 * a_cols + b_cols].reshape(bs, ts, b_cols),
                               proj_s[:, kb_cols].reshape(bs, ts, b_cols), proj_s[:, vb_cols].reshape(bs, ts, b_cols),
                               cache_k, cache_v, i, page_table, n_heads=h_b, dh=dh_b)
            oa = jnp.concatenate([oa_p, _unpad_seq(oa_s, bs, ts, ts_ret)], axis=0)
            ob = jnp.concatenate([ob_p, ob_s.reshape(ms, b_cols).astype(BF16)], axis=0)
            x = out_proj_residual(oa, ob, wout_ab, i, x)
            ret_p.append(s_p)
            ret_s.append(s_s)
        else:
            proj = in_proj(hn, win_cd, i, F32)
            proj_s = proj[mp:]
            xb_p = proj[:mp, :w_c].reshape(bp, tp, w_c)
            xb_s = proj_s[:, :w_c].reshape(bs, ts, w_c)
            buf_p.append(jnp.concatenate([zeros(bp, taps - 1, w_c), xb_p], axis=1)[:, tp:])
            buf_s.append(jnp.concatenate([state_conv[i], xb_s], axis=1)[:, ts:])

            lru = functools.partial(rglru, conv_w=conv_w[i], conv_b=conv_b[i], w_r=w_rgate[i], b_r=b_rgate[i],
                                    w_i=w_igate[i], b_i=b_igate[i], lam=lru_lambda[i], width=w_c)
            oc_p, hl_p = lru(proj, 0, bp, tp, zeros(bp, taps - 1, w_c), zeros(bp, w_c), n_valid=min(tp, 256), tc=min(tp, 256))
            oc_s, hl_s = lru(_pad_seq(proj_s[:, :2 * w_c], bs, ts, ts_lru), 0, bs, ts_lru, state_conv[i], state_rglru[i],
                             n_valid=ts, tc=ts_lru)
            hg = functools.partial(hgrn2, lb_logits=hgrn_lb_logits, norm_g=hgrn_norm[i], layer=i, n_heads=h_d,
                                   dk=dk_d, dv=dv_d, col0=2 * w_c // dk_d, chunk=128)
            od_p, sg_p = hg(proj, 0, bp, tp, s0=zeros(bp, h_d, dk_d, dv_d), n_valid=128)
            od_s, sg_s = hg(_pad_seq(proj_s, bs, ts, ts_hg), 0, bs, ts_hg, s0=state_hgrn[i], n_valid=ts)
            oc = jnp.concatenate([oc_p, _unpad_seq(oc_s, bs, ts, ts_lru)], axis=0)
            od = jnp.concatenate([od_p, _unpad_seq(od_s, bs, ts, ts_hg)], axis=0)
            x = out_proj_residual(oc, od, wout_cd, i, x)
            h_p.append(hl_p.reshape(bp, w_c))
            h_s.append(hl_s.reshape(bs, w_c))
            hg_p.append(sg_p)
            hg_s.append(sg_s)
        x = ffn(x, l, 1, norm_ffn2[l])
    y = rmsnorm(x, final_norm, F32)
    st = jnp.stack
    return (y[:mp].reshape(bp, tp, d_model), y[mp:].reshape(bs, ts, d_model),
            st(ret_p), st(k_p), st(v_p), st(h_p), st(buf_p), st(hg_p),
            st(ret_s), st(k_s), st(v_s), st(h_s), st(buf_s), st(hg_s))
```

```python
import functools

import jax
import jax.numpy as jnp
from jax import lax
from jax.experimental import pallas as pl
from jax.experimental.pallas import tpu as pltpu

F32 = jnp.float32
BF16 = jnp.bfloat16

NORM_EPS = 1e-6
ROPE_BASE = 10000.0
MOBA_BLOCK = 256
MOBA_TOPK = 3
RG_C = 8.0
GATE_BLOCK = 128
SUB = 16
HGRN_HEADS_PER_STEP = 4
MIB = 1024 * 1024
ROW_BLOCK = 1024
COL_BLOCK = 512

_NT = (((1,), (1,)), ((), ()))
_TN = (((0,), (0,)), ((), ()))


def _params(semantics, vmem_mib):
    return pltpu.CompilerParams(dimension_semantics=semantics,
                                vmem_limit_bytes=int(vmem_mib * MIB))


def _bdot(a, b):
    return jnp.dot(a.astype(BF16), b.astype(BF16), preferred_element_type=F32)


def _bdot_nt(a, b):
    return lax.dot_general(a.astype(BF16), b.astype(BF16), _NT, preferred_element_type=F32)


def _bdot_tn(a, b):
    return lax.dot_general(a.astype(BF16), b.astype(BF16), _TN, preferred_element_type=F32)


def _silu(x):
    return x * jax.nn.sigmoid(x)


def _rms_kernel(x_ref, g_ref, o_ref):
    x = x_ref[...]
    ms = jnp.mean(x * x, axis=-1, keepdims=True)
    o_ref[...] = (x * lax.rsqrt(ms + NORM_EPS) * g_ref[...]).astype(o_ref.dtype)


def rmsnorm(x, g, out_dtype):
    m, d = x.shape
    bm = min(m, 256)
    return pl.pallas_call(
        _rms_kernel,
        grid=(pl.cdiv(m, bm),),
        in_specs=[pl.BlockSpec((bm, d), lambda i: (i, 0)),
                  pl.BlockSpec((1, d), lambda i: (0, 0))],
        out_specs=pl.BlockSpec((bm, d), lambda i: (i, 0)),
        out_shape=jax.ShapeDtypeStruct((m, d), out_dtype),
        compiler_params=_params(("parallel",), 32),
        name="rmsnorm",
    )(x, g.reshape(1, d))


def _ahead(j, i, n_j, n_i, lead):
    lead = min(lead, n_i - 1)
    if lead <= 0:
        return j
    return jnp.minimum(j + (i >= n_i - lead).astype(jnp.int32), n_j - 1)


def _round_into(dst, srcs):
    r0 = 0
    for src in srcs:
        dst[r0:r0 + src.shape[0], :] = src[...].astype(BF16)
        r0 += src.shape[0]


def _weight_specs(kparts, bn, lead0, n_j, n_i, prefix, krow0=0):
    specs = []
    for p, kp in enumerate(kparts):
        def imap(j, i, p=p):
            return prefix + (krow0 + p, _ahead(j, i, n_j, n_i, lead0 + p))
        specs.append(pl.BlockSpec((None,) * len(prefix) + (kp, bn), imap))
    return specs


def _mm_kernel(xp_ref, xs_ref, w0_ref, w1_ref, op_ref, os_ref, wb):
    @pl.when(pl.program_id(1) == 0)
    def _():
        _round_into(wb, [w0_ref, w1_ref])
        os_ref[...] = jnp.dot(xs_ref[...], wb[...], preferred_element_type=F32).astype(os_ref.dtype)

    op_ref[...] = jnp.dot(xp_ref[...], wb[...], preferred_element_type=F32).astype(op_ref.dtype)


def in_proj(xp, xs, w_all, idx, out_dtype):
    mp, k = xp.shape
    ms = xs.shape[0]
    n = w_all.shape[-1]
    bm, bn = min(mp, ROW_BLOCK), COL_BLOCK
    n_j, n_i = pl.cdiv(n, bn), mp // bm
    assert mp % bm == 0 and k % 2 == 0
    return pl.pallas_call(
        _mm_kernel,
        grid=(n_j, n_i),
        in_specs=[pl.BlockSpec((bm, k), lambda j, i: (i, 0)),
                  pl.BlockSpec((ms, k), lambda j, i: (0, 0))]
                 + _weight_specs((k // 2, k // 2), bn, 2, n_j, n_i, (idx,)),
        out_specs=[pl.BlockSpec((bm, bn), lambda j, i: (i, j)),
                   pl.BlockSpec((ms, bn), lambda j, i: (0, j))],
        out_shape=[jax.ShapeDtypeStruct((mp, n), out_dtype), jax.ShapeDtypeStruct((ms, n), out_dtype)],
        scratch_shapes=[pltpu.VMEM((k, bn), BF16)],
        compiler_params=_params(("arbitrary", "arbitrary"), 52),
        name="in_proj",
    )(xp, xs, w_all, w_all)


def _gate_up_kernel(xp_ref, xs_ref, wg_ref, wu_ref, op_ref, os_ref, wgb, wub):
    def act(x):
        g = jnp.dot(x, wgb[...], preferred_element_type=F32)
        u = jnp.dot(x, wub[...], preferred_element_type=F32)
        return (_silu(g) * u).astype(op_ref.dtype)

    @pl.when(pl.program_id(1) == 0)
    def _():
        _round_into(wgb, [wg_ref])
        _round_into(wub, [wu_ref])
        os_ref[...] = act(xs_ref[...])

    op_ref[...] = act(xp_ref[...])


def gate_up(xp, xs, wg_all, wu_all, l, j2):
    mp, k = xp.shape
    ms = xs.shape[0]
    n = wg_all.shape[-1]
    bm, bn = min(mp, ROW_BLOCK), COL_BLOCK // 2
    n_j, n_i = pl.cdiv(n, bn), mp // bm
    assert mp % bm == 0
    return pl.pallas_call(
        _gate_up_kernel,
        grid=(n_j, n_i),
        in_specs=[pl.BlockSpec((bm, k), lambda j, i: (i, 0)),
                  pl.BlockSpec((ms, k), lambda j, i: (0, 0))]
                 + _weight_specs((k,), bn, 3, n_j, n_i, (l, j2))
                 + _weight_specs((k,), bn, 2, n_j, n_i, (l, j2)),
        out_specs=[pl.BlockSpec((bm, bn), lambda j, i: (i, j)),
                   pl.BlockSpec((ms, bn), lambda j, i: (0, j))],
        out_shape=[jax.ShapeDtypeStruct((mp, n), BF16), jax.ShapeDtypeStruct((ms, n), BF16)],
        scratch_shapes=[pltpu.VMEM((k, bn), BF16), pltpu.VMEM((k, bn), BF16)],
        compiler_params=_params(("arbitrary", "arbitrary"), 52),
        name="ffn_gate_up",
    )(xp, xs, wg_all, wu_all)


def _down_kernel(hp_ref, hs_ref, w0_ref, w1_ref, xp_ref, xs_ref, op_ref, os_ref, wb, *, scale):
    @pl.when(pl.program_id(1) == 0)
    def _():
        _round_into(wb, [w0_ref, w1_ref])
        os_ref[...] = xs_ref[...] + scale * jnp.dot(hs_ref[...], wb[...], preferred_element_type=F32)

    op_ref[...] = xp_ref[...] + scale * jnp.dot(hp_ref[...], wb[...], preferred_element_type=F32)


def down_residual(hp, hs, w_all, l, j2, xp, xs, scale):
    mp, k = hp.shape
    ms = hs.shape[0]
    n = w_all.shape[-1]
    tk = k // 2
    assert tk * 2 == k and tk % 128 == 0 and (tk // 2) % 8 == 0
    bm, bn = min(mp, ROW_BLOCK // 2), COL_BLOCK
    n_j, n_i = n // bn, mp // bm
    assert mp % bm == 0 and n % bn == 0
    for kb in range(2):
        xp, xs = pl.pallas_call(
            functools.partial(_down_kernel, scale=scale),
            grid=(n_j, n_i),
            in_specs=[pl.BlockSpec((bm, tk), lambda j, i, kb=kb: (i, kb)),
                      pl.BlockSpec((ms, tk), lambda j, i, kb=kb: (0, kb))]
                     + _weight_specs((tk // 2, tk // 2), bn, 2, n_j, n_i, (l, j2), krow0=2 * kb)
                     + [pl.BlockSpec((bm, bn), lambda j, i: (i, j)),
                        pl.BlockSpec((ms, bn), lambda j, i: (0, j))],
            out_specs=[pl.BlockSpec((bm, bn), lambda j, i: (i, j)),
                       pl.BlockSpec((ms, bn), lambda j, i: (0, j))],
            out_shape=[jax.ShapeDtypeStruct((mp, n), F32), jax.ShapeDtypeStruct((ms, n), F32)],
            scratch_shapes=[pltpu.VMEM((tk, bn), BF16)],
            compiler_params=_params(("arbitrary", "arbitrary"), 54),
            name="ffn_down",
        )(hp, hs, w_all, w_all, xp, xs)
    return xp, xs


def _out_proj_kernel(ap_ref, bp_ref, as_ref, bs_ref, wa_ref, wb_ref, xp_ref, xs_ref, op_ref, os_ref, wab, wbb):
    def mix(a, b):
        return (jnp.dot(a, wab[...], preferred_element_type=F32)
                + jnp.dot(b, wbb[...], preferred_element_type=F32))

    @pl.when(pl.program_id(1) == 0)
    def _():
        _round_into(wab, [wa_ref])
        _round_into(wbb, [wb_ref])
        os_ref[...] = xs_ref[...] + mix(as_ref[...], bs_ref[...])

    op_ref[...] = xp_ref[...] + mix(ap_ref[...], bp_ref[...])


def out_proj_residual(ap, bp, as_, bs, w_all, idx, xp, xs):
    mp, ka = ap.shape
    ms = as_.shape[0]
    assert bp.shape[1] == ka and w_all.shape[1] == 2 * ka
    n = w_all.shape[-1]
    bm, bn = min(mp, ROW_BLOCK), COL_BLOCK
    n_j, n_i = n // bn, mp // bm
    assert mp % bm == 0 and n % bn == 0
    row = lambda r, c: pl.BlockSpec((r, c), lambda j, i: (i, 0))
    fix = lambda r, c: pl.BlockSpec((r, c), lambda j, i: (0, 0))
    wa_spec, wb_spec = _weight_specs((ka, ka), bn, 2, n_j, n_i, (idx,))
    return pl.pallas_call(
        _out_proj_kernel,
        grid=(n_j, n_i),
        in_specs=[row(bm, ka), row(bm, ka), fix(ms, ka), fix(ms, ka), wa_spec, wb_spec,
                  pl.BlockSpec((bm, bn), lambda j, i: (i, j)),
                  pl.BlockSpec((ms, bn), lambda j, i: (0, j))],
        out_specs=[pl.BlockSpec((bm, bn), lambda j, i: (i, j)),
                   pl.BlockSpec((ms, bn), lambda j, i: (0, j))],
        out_shape=[jax.ShapeDtypeStruct((mp, n), F32), jax.ShapeDtypeStruct((ms, n), F32)],
        scratch_shapes=[pltpu.VMEM((ka, bn), BF16), pltpu.VMEM((ka, bn), BF16)],
        compiler_params=_params(("arbitrary", "arbitrary"), 54),
        name="out_proj",
    )(ap, bp, as_, bs, w_all, w_all, xp, xs)


def _cumsum_rows(x):
    n = x.shape[0]
    row = lax.broadcasted_iota(jnp.int32, x.shape, 0)
    s = 1
    while s < n:
        x = x + jnp.where(row >= s, pltpu.roll(x, s, 0), 0.0)
        s *= 2
    return x


def _linear_scan_rows(a, u):
    n = a.shape[0]
    row = lax.broadcasted_iota(jnp.int32, a.shape, 0)
    s = 1
    while s < n:
        keep = row >= s
        a_prev = jnp.where(keep, pltpu.roll(a, s, 0), 1.0)
        u_prev = jnp.where(keep, pltpu.roll(u, s, 0), 0.0)
        u = a * u_prev + u
        a = a * a_prev
        s *= 2
    return a, u


def _retention_kernel(q_ref, k_ref, v_ref, g_ref, cos_ref, sin_ref, lg_ref, gn_ref, s0_ref,
                      o_ref, sout_ref, s_scr, dec_scr, *, chunk, n_valid, dk):
    c = pl.program_id(2)
    log_g = lg_ref[0][:, :1]

    @pl.when(c == 0)
    def _():
        s_scr[...] = s0_ref[0, 0]
        i = lax.broadcasted_iota(jnp.int32, (chunk, chunk), 0)
        j = lax.broadcasted_iota(jnp.int32, (chunk, chunk), 1)
        d = (i - j).astype(F32) * log_g
        dec_scr[...] = jnp.exp(jnp.where(i >= j, d, -jnp.inf))

    half = dk // 2
    cos = cos_ref[...]
    sin = sin_ref[...]

    def rot(x):
        x1 = x[:, :half]
        x2 = x[:, half:]
        return jnp.concatenate([x1 * cos - x2 * sin, x1 * sin + x2 * cos], axis=-1)

    q = rot(q_ref[...])
    k = rot(k_ref[...]) * (dk ** -0.5)
    v = v_ref[...]
    idx = lax.broadcasted_iota(jnp.int32, (chunk, 1), 0).astype(F32)
    inter = jnp.exp((idx + 1.0) * log_g)
    to_state = jnp.exp((n_valid - 1.0 - idx) * log_g)
    carry = jnp.exp(n_valid * log_g)

    s = s_scr[...]
    att = _bdot_nt(q, k) * dec_scr[...]
    o = _bdot(att, v) + _bdot(q, s) * inter
    s_new = carry * s + _bdot_tn(k * to_state, v)
    s_scr[...] = s_new

    mu = jnp.mean(o, axis=-1, keepdims=True)
    oc = o - mu
    var = jnp.mean(oc * oc, axis=-1, keepdims=True)
    y = oc * lax.rsqrt(var + NORM_EPS) * gn_ref[...]
    o_ref[...] = (y * _silu(g_ref[...])).astype(o_ref.dtype)

    @pl.when(c == pl.num_programs(2) - 1)
    def _():
        sout_ref[0, 0] = s_new


def retention(proj, row0, b, t, cos, sin, gn, s0, *, n_heads, dk, dv, n_valid, chunk):
    assert dk == dv and t % chunk == 0 and row0 % chunk == 0 and (n_valid == chunk or t == chunk)
    h = n_heads
    nc = t // chunk
    rb0 = row0 // chunk
    log_g = jnp.log1p(-jnp.exp2(-5.0 - jnp.arange(h, dtype=F32)))
    lg = jnp.broadcast_to(log_g[:, None, None], (h, 1, 128))
    kern = functools.partial(_retention_kernel, chunk=chunk, n_valid=n_valid, dk=dk)
    col = lambda off: pl.BlockSpec((chunk, dk), lambda bi, hi, ci, off=off: (rb0 + bi * nc + ci, off + hi))
    return pl.pallas_call(
        kern,
        grid=(b, h, nc),
        in_specs=[col(0), col(h), col(2 * h), col(3 * h),
                  pl.BlockSpec((chunk, dk // 2), lambda bi, hi, ci: (ci, 0)),
                  pl.BlockSpec((chunk, dk // 2), lambda bi, hi, ci: (ci, 0)),
                  pl.BlockSpec((1, 1, 128), lambda bi, hi, ci: (hi, 0, 0)),
                  pl.BlockSpec((1, dv), lambda bi, hi, ci: (0, hi)),
                  pl.BlockSpec((1, 1, dk, dv), lambda bi, hi, ci: (bi, hi, 0, 0))],
        out_specs=[pl.BlockSpec((chunk, dv), lambda bi, hi, ci: (bi * nc + ci, hi)),
                   pl.BlockSpec((1, 1, dk, dv), lambda bi, hi, ci: (bi, hi, 0, 0))],
        out_shape=[jax.ShapeDtypeStruct((b * t, h * dv), BF16),
                   jax.ShapeDtypeStruct((b, h, dk, dv), F32)],
        scratch_shapes=[pltpu.VMEM((dk, dv), F32), pltpu.VMEM((chunk, chunk), F32)],
        compiler_params=_params(("parallel", "parallel", "arbitrary"), 32),
        name="retention",
    )(proj, proj, proj, proj, cos, sin, lg, gn.reshape(1, h * dv), s0)


def _moba_prompt_kernel(q_ref, k_ref, v_ref, o_ref, kmean_scr, kb_scr, vb_scr, *, nb, scale):
    i = pl.program_id(2)
    blk = MOBA_BLOCK

    @pl.when(i == 0)
    def _():
        kk = k_ref[...]
        kmean_scr[...] = jnp.mean(kk.reshape(nb, blk, kk.shape[-1]), axis=1)
        kb_scr[...] = kk.astype(BF16)
        vb_scr[...] = v_ref[...].astype(BF16)

    q = q_ref[...]
    qb = q.astype(BF16)
    gate = lax.dot_general(kmean_scr[...], q, _NT, precision=lax.Precision.HIGHEST,
                           preferred_element_type=F32)
    bidx = lax.broadcasted_iota(jnp.int32, (nb, blk), 0)
    past = bidx < i
    gate = jnp.where(past, gate, -jnp.inf)
    rank = jnp.zeros(gate.shape, jnp.int32)
    for mm in range(nb):
        gm = gate[mm:mm + 1, :]
        rank = rank + ((gm > gate) | ((gm == gate) & (bidx > mm))).astype(jnp.int32)
    sel = jnp.where((rank < MOBA_TOPK) & past, 1.0, 0.0).T

    row = lax.broadcasted_iota(jnp.int32, (blk, blk), 0)
    colk = lax.broadcasted_iota(jnp.int32, (blk, blk), 1)
    causal = colk <= row

    def attend(nkb):
        s = _bdot_nt(qb, kb_scr[:nkb * blk, :]) * scale
        pieces = []
        for n in range(nkb):
            own = jnp.full((blk, blk), n, jnp.int32) == i
            allowed = (jnp.broadcast_to(sel[:, n:n + 1], (blk, blk)) > 0.5) | (own & causal)
            pieces.append(jnp.where(allowed, s[:, n * blk:(n + 1) * blk], -jnp.inf))
        s = jnp.concatenate(pieces, axis=-1)
        m = jnp.max(s, axis=-1, keepdims=True)
        p = jnp.exp(s - m)
        l = jnp.sum(p, axis=-1, keepdims=True)
        o_ref[...] = (_bdot(p, vb_scr[:nkb * blk, :]) / l).astype(o_ref.dtype)

    step = 2 if nb % 2 == 0 else 1
    for nkb in range(step, nb + 1, step):
        @pl.when((i >= nkb - step) & (i < nkb))
        def _(nkb=nkb):
            attend(nkb)


def moba_prompt(proj, b, t, *, n_heads, dh, q_off, k_off, v_off):
    assert t % MOBA_BLOCK == 0
    nb = t // MOBA_BLOCK
    kern = functools.partial(_moba_prompt_kernel, nb=nb, scale=dh ** -0.5)
    return pl.pallas_call(
        kern,
        grid=(b, n_heads, nb),
        in_specs=[pl.BlockSpec((MOBA_BLOCK, dh), lambda bi, hi, ii: (bi * nb + ii, q_off + hi)),
                  pl.BlockSpec((t, dh), lambda bi, hi, ii: (bi, k_off + hi)),
                  pl.BlockSpec((t, dh), lambda bi, hi, ii: (bi, v_off + hi))],
        out_specs=pl.BlockSpec((MOBA_BLOCK, dh), lambda bi, hi, ii: (bi * nb + ii, hi)),
        out_shape=jax.ShapeDtypeStruct((b * t, n_heads * dh), BF16),
        scratch_shapes=[pltpu.VMEM((nb, dh), F32), pltpu.VMEM((t, dh), BF16), pltpu.VMEM((t, dh), BF16)],
        compiler_params=_params(("parallel", "parallel", "arbitrary"), 40),
        name="moba_prompt",
    )(proj, proj, proj)


def _moba_kmean_kernel(pt_ref, *refs):
    *k_refs, o_ref = refs
    acc = jnp.sum(k_refs[0][...], axis=0)
    for k_ref in k_refs[1:]:
        acc = acc + jnp.sum(k_ref[...], axis=0)
    o_ref[...] = acc * (1.0 / MOBA_BLOCK)


def _moba_decode_kernel(pt_ref, q_ref, kmean_ref, knew_ref, vnew_ref, *refs, n_heads, nblk, ts, ppb, scale):
    k_refs, v_refs = refs[:ppb], refs[ppb:2 * ppb]
    o_ref, sel_scr, bias_scr, m_scr, l_scr, acc_scr = refs[2 * ppb:]
    n = pl.program_id(1)
    q = q_ref[0]
    rows, dh = q.shape
    qb = q.astype(BF16)

    @pl.when(n == 0)
    def _():
        r = lax.broadcasted_iota(jnp.int32, bias_scr.shape, 0)
        c = lax.broadcasted_iota(jnp.int32, bias_scr.shape, 1)
        bias_scr[...] = jnp.where((r % n_heads) == (c % n_heads), 0.0, -jnp.inf)

        km = kmean_ref[0].reshape(nblk * n_heads, dh)
        g = lax.dot_general(q, km, _NT, precision=lax.Precision.HIGHEST, preferred_element_type=F32)
        width = nblk * n_heads
        nidx = lax.broadcasted_iota(jnp.int32, (rows, width), 1) // n_heads
        rank = jnp.zeros((rows, width), jnp.int32)
        for k in range(1, nblk):
            other = pltpu.roll(g, k * n_heads, 1)
            rank = rank + ((other > g) | ((other == g) & (nidx >= k))).astype(jnp.int32)
        sel_scr[...] = jnp.where(rank < MOBA_TOPK, 1.0, 0.0)

        kn = knew_ref[0].reshape(-1, dh)
        vn = vnew_ref[0].reshape(-1, dh)
        r = lax.broadcasted_iota(jnp.int32, (rows, kn.shape[0]), 0)
        c = lax.broadcasted_iota(jnp.int32, (rows, kn.shape[0]), 1)
        tj = c // n_heads
        ok = ((r % n_heads) == (c % n_heads)) & (tj <= r // n_heads) & (tj < ts)
        s = jnp.where(ok, _bdot_nt(qb, kn) * scale, -jnp.inf)
        m0 = jnp.max(s, axis=-1, keepdims=True)
        e = jnp.exp(s - m0)
        m_scr[...] = m0
        l_scr[...] = jnp.sum(e, axis=-1, keepdims=True)
        acc_scr[...] = _bdot(e, vn)

    rs = lax.broadcasted_iota(jnp.int32, sel_scr.shape, 0)
    cs = lax.broadcasted_iota(jnp.int32, sel_scr.shape, 1)
    flag = jnp.sum(jnp.where(cs == n * n_heads + rs % n_heads, sel_scr[...], 0.0), axis=-1, keepdims=True)
    bias = bias_scr[...] + jnp.where(flag > 0.5, 0.0, -jnp.inf)
    scores = [_bdot_nt(qb, k_ref[...].reshape(-1, dh)) * scale + bias for k_ref in k_refs]
    m_old = m_scr[...]
    m_new = m_old
    for s in scores:
        m_new = jnp.maximum(m_new, jnp.max(s, axis=-1, keepdims=True))
    alpha = jnp.exp(m_old - m_new)
    l = alpha * l_scr[...]
    acc = alpha * acc_scr[...]
    for s, v_ref in zip(scores, v_refs):
        e = jnp.exp(s - m_new)
        l = l + jnp.sum(e, axis=-1, keepdims=True)
        acc = acc + _bdot(e, v_ref[...].reshape(-1, dh))
    m_scr[...] = m_new
    l_scr[...] = l
    acc_scr[...] = acc

    @pl.when(n == pl.num_programs(1) - 1)
    def _():
        o_ref[0] = acc / l


def moba_sample(qb, knew, vnew, cache_k, cache_v, layer, page_table, *, n_heads, dh):
    b, ts, w = qb.shape
    n_pages = page_table.shape[1]
    page = cache_k.shape[2]
    past = n_pages * page
    assert past % MOBA_BLOCK == 0 and MOBA_BLOCK % page == 0 and ts <= MOBA_BLOCK and w == n_heads * dh
    ppb = MOBA_BLOCK // page
    nblk = past // MOBA_BLOCK
    rows = ts * n_heads
    ts_pad = -(-ts // 8) * 8
    pad = ((0, 0), (0, ts_pad - ts), (0, 0), (0, 0))
    q3 = qb.reshape(b, rows, dh)
    kn = jnp.pad(knew.reshape(b, ts, n_heads, dh), pad)
    vn = jnp.pad(vnew.reshape(b, ts, n_heads, dh), pad)
    page_specs = [pl.BlockSpec((None, None, page, n_heads, dh),
                               lambda bi, ni, pt, pg=pg: (layer, pt[bi, ni * ppb + pg], 0, 0, 0))
                  for pg in range(ppb)]

    kmean = pl.pallas_call(
        _moba_kmean_kernel,
        grid_spec=pltpu.PrefetchScalarGridSpec(
            num_scalar_prefetch=1,
            grid=(b, nblk),
            in_specs=page_specs,
            out_specs=pl.BlockSpec((None, None, n_heads, dh), lambda bi, ni, pt: (bi, ni, 0, 0))),
        out_shape=jax.ShapeDtypeStruct((b, nblk, n_heads, dh), F32),
        compiler_params=_params(("parallel", "arbitrary"), 32),
        name="moba_sample_kmean",
    )(page_table, *([cache_k] * ppb))

    out = pl.pallas_call(
        functools.partial(_moba_decode_kernel, n_heads=n_heads, nblk=nblk, ts=ts, ppb=ppb, scale=dh ** -0.5),
        grid_spec=pltpu.PrefetchScalarGridSpec(
            num_scalar_prefetch=1,
            grid=(b, nblk),
            in_specs=[pl.BlockSpec((1, rows, dh), lambda bi, ni, pt: (bi, 0, 0)),
                      pl.BlockSpec((1, nblk, n_heads, dh), lambda bi, ni, pt: (bi, 0, 0, 0)),
                      pl.BlockSpec((1, ts_pad, n_heads, dh), lambda bi, ni, pt: (bi, 0, 0, 0)),
                      pl.BlockSpec((1, ts_pad, n_heads, dh), lambda bi, ni, pt: (bi, 0, 0, 0))]
                     + page_specs + page_specs,
            out_specs=pl.BlockSpec((1, rows, dh), lambda bi, ni, pt: (bi, 0, 0)),
            scratch_shapes=[pltpu.VMEM((rows, nblk * n_heads), F32), pltpu.VMEM((rows, page * n_heads), F32),
                            pltpu.VMEM((rows, 1), F32), pltpu.VMEM((rows, 1), F32), pltpu.VMEM((rows, dh), F32)]),
        out_shape=jax.ShapeDtypeStruct((b, rows, dh), F32),
        compiler_params=_params(("parallel", "arbitrary"), 40),
        name="moba_sample_attn",
    )(page_table, q3, kmean, kn, vn, *([cache_k] * ppb), *([cache_v] * ppb))
    return out.reshape(b, ts, w)


def _rglru_kernel(x_ref, g_ref, tail0_ref, h0_ref, cw_ref, cb_ref, wr_ref, br_ref, wi_ref, bi_ref, lam_ref,
                  o_ref, hlast_ref, tail_scr, h_scr, *, tc, n_valid):
    c = pl.program_id(2)

    @pl.when(c == 0)
    def _():
        tail_scr[...] = tail0_ref[0]
        h_scr[...] = h0_ref[0]

    x = x_ref[...]
    cw = cw_ref[...]
    taps = cw.shape[0]
    ext = jnp.concatenate([tail_scr[...], x], axis=0)
    xc = cb_ref[...] + x * cw[taps - 1:taps, :]
    for s in range(1, taps):
        xc = xc + pltpu.roll(ext, s, 0)[8:, :] * cw[taps - 1 - s:taps - s, :]
    tail_scr[...] = x[tc - 8:, :]

    nblk = x.shape[1] // GATE_BLOCK
    zr, zi = [], []
    for kb in range(nblk):
        xb = xc[:, kb * GATE_BLOCK:(kb + 1) * GATE_BLOCK]
        zr.append(_bdot(xb, wr_ref[kb]))
        zi.append(_bdot(xb, wi_ref[kb]))
    r = jax.nn.sigmoid(jnp.concatenate(zr, axis=-1) + br_ref[...])
    ig = jax.nn.sigmoid(jnp.concatenate(zi, axis=-1) + bi_ref[...])
    lam = lam_ref[...]
    softplus_neg = jnp.maximum(-lam, 0.0) + jnp.log1p(jnp.exp(-jnp.abs(lam)))
    log_a = -RG_C * r * softplus_neg
    a = jnp.exp(log_a)
    u = jnp.sqrt(jnp.tanh(-log_a) * (1.0 + a * a)) * ig * xc
    a_cum, hz = _linear_scan_rows(a, u)
    h = hz + a_cum * h_scr[...]
    h_scr[...] = h[n_valid - 1:n_valid, :]
    o_ref[...] = (h * jax.nn.gelu(g_ref[...])).astype(o_ref.dtype)

    @pl.when(c == pl.num_programs(2) - 1)
    def _():
        hlast_ref[0] = h[n_valid - 1:n_valid, :]


def rglru(proj, row0, b, t, conv_buf, h0, conv_w, conv_b, w_r, b_r, w_i, b_i, lam, *, width, n_valid, tc, bw=512):
    taps = conv_w.shape[0]
    assert t % tc == 0 and tc % 8 == 0 and row0 % tc == 0 and (n_valid == tc or t == tc) and taps - 1 <= 8
    nw = width // bw
    gpb = bw // GATE_BLOCK
    nc = t // tc
    rb0 = row0 // tc
    tail0 = jnp.pad(conv_buf, ((0, 0), (8 - (taps - 1), 0), (0, 0)))
    vec = lambda a: a.reshape(1, width)
    vspec = pl.BlockSpec((1, bw), lambda bi, wi, ci: (0, wi))
    kern = functools.partial(_rglru_kernel, tc=tc, n_valid=n_valid)
    return pl.pallas_call(
        kern,
        grid=(b, nw, nc),
        in_specs=[pl.BlockSpec((tc, bw), lambda bi, wi, ci: (rb0 + bi * nc + ci, wi)),
                  pl.BlockSpec((tc, bw), lambda bi, wi, ci: (rb0 + bi * nc + ci, nw + wi)),
                  pl.BlockSpec((1, 8, bw), lambda bi, wi, ci: (bi, 0, wi)),
                  pl.BlockSpec((1, 1, bw), lambda bi, wi, ci: (bi, 0, wi)),
                  pl.BlockSpec((taps, bw), lambda bi, wi, ci: (0, wi)),
                  vspec,
                  pl.BlockSpec((gpb, GATE_BLOCK, GATE_BLOCK), lambda bi, wi, ci: (wi, 0, 0)),
                  vspec,
                  pl.BlockSpec((gpb, GATE_BLOCK, GATE_BLOCK), lambda bi, wi, ci: (wi, 0, 0)),
                  vspec, vspec],
        out_specs=[pl.BlockSpec((tc, bw), lambda bi, wi, ci: (bi * nc + ci, wi)),
                   pl.BlockSpec((1, 1, bw), lambda bi, wi, ci: (bi, 0, wi))],
        out_shape=[jax.ShapeDtypeStruct((b * t, width), BF16),
                   jax.ShapeDtypeStruct((b, 1, width), F32)],
        scratch_shapes=[pltpu.VMEM((8, bw), F32), pltpu.VMEM((1, bw), F32)],
        compiler_params=_params(("parallel", "parallel", "arbitrary"), 32),
        name="rglru",
    )(proj, proj, tail0, h0.reshape(b, 1, width), conv_w, vec(conv_b), w_r, vec(b_r), w_i, vec(b_i), vec(lam))


def _hgrn_head(q, fz, v, g, lb, ng, st, *, chunk, n_valid):
    dk = q.shape[1]
    log_f = jnp.log(lb + (1.0 - lb) * jax.nn.sigmoid(fz))
    key = (1.0 - lb) * jax.nn.sigmoid(-fz)
    if n_valid < chunk:
        live = lax.broadcasted_iota(jnp.int32, (chunk, dk), 0) < n_valid
        log_f = jnp.where(live, log_f, 0.0)
        key = jnp.where(live, key, 0.0)
    cum = _cumsum_rows(log_f)
    last = cum[chunk - 1:chunk, :]

    o = _bdot_nt(q * jnp.exp(cum), st)

    row = lax.broadcasted_iota(jnp.int32, (chunk, dk), 0)
    ri = lax.broadcasted_iota(jnp.int32, (chunk, chunk), 0)
    ci = lax.broadcasted_iota(jnp.int32, (chunk, chunk), 1)
    att = None
    s = chunk // 2
    while s >= SUB:
        grp = chunk // (2 * s)
        edge = jnp.broadcast_to(cum.reshape(grp, 2 * s, dk)[:, s - 1:s, :], (grp, 2 * s, dk)).reshape(chunk, dk)
        upper = (row % (2 * s)) >= s
        qs = q * jnp.exp(jnp.where(upper, cum - edge, -jnp.inf))
        ks = key * jnp.exp(jnp.where(upper, -jnp.inf, edge - cum))
        a = _bdot_nt(qs, ks)
        if grp > 1:
            a = jnp.where((ri // (2 * s)) == (ci // (2 * s)), a, 0.0)
        att = a if att is None else att + a
        s //= 2
    if att is not None:
        o = o + _bdot(att, v)

    nsb = chunk // SUB
    q3 = q.reshape(nsb, SUB, dk)
    k3 = key.reshape(nsb, SUB, dk)
    c3 = cum.reshape(nsb, SUB, dk)
    v3 = v.reshape(nsb, SUB, v.shape[1])
    ri = lax.broadcasted_iota(jnp.int32, (nsb, SUB, dk), 1)
    od = jnp.zeros(v3.shape, F32)
    for j in range(SUB):
        d = jnp.where(ri >= j, c3 - c3[:, j:j + 1, :], -jnp.inf)
        wgt = jnp.sum(q3 * jnp.exp(d) * k3[:, j:j + 1, :], axis=-1, keepdims=True)
        od = od + wgt * v3[:, j:j + 1, :]
    o = o + od.reshape(chunk, v.shape[1])

    st_new = st * jnp.exp(last) + _bdot_tn(v, key * jnp.exp(last - cum))
    y = o * lax.rsqrt(jnp.mean(o * o, axis=-1, keepdims=True) + NORM_EPS) * ng
    return y * _silu(g), st_new


def _hgrn_kernel(q_ref, f_ref, i_ref, g_ref, lbl_ref, ng_ref, s0_ref, o_ref, sout_ref, st_scr,
                 *, chunk, n_valid, layer, hp, dk):
    c = pl.program_id(2)

    @pl.when(c == 0)
    def _():
        for k in range(hp):
            st_scr[k] = s0_ref[0, k].T

    lbl = lbl_ref[...]
    e = jnp.exp(lbl - jnp.max(lbl, axis=0, keepdims=True))
    soft = e / jnp.sum(e, axis=0, keepdims=True)
    lb_all = jnp.zeros((1, lbl.shape[1]), F32)
    for r in range(1, layer + 1):
        lb_all = lb_all + soft[r:r + 1, :]

    outs, states = [], []
    for k in range(hp):
        sl = slice(k * dk, (k + 1) * dk)
        y, st_new = _hgrn_head(q_ref[:, sl], f_ref[:, sl], i_ref[:, sl], g_ref[:, sl], lb_all[:, sl],
                               ng_ref[:, sl], st_scr[k], chunk=chunk, n_valid=n_valid)
        st_scr[k] = st_new
        outs.append(y)
        states.append(st_new)
    o_ref[...] = jnp.concatenate(outs, axis=-1).astype(o_ref.dtype)

    @pl.when(c == pl.num_programs(2) - 1)
    def _():
        for k in range(hp):
            sout_ref[0, k] = states[k].T


def hgrn2(proj, row0, b, t, lb_logits, norm_g, s0, *, layer, n_heads, dk, dv, col0, n_valid, chunk=128):
    hp = HGRN_HEADS_PER_STEP
    assert dk == dv and t % chunk == 0 and chunk % SUB == 0 and row0 % chunk == 0 and (n_valid == chunk or t == chunk)
    assert n_heads % hp == 0 and col0 % hp == 0
    h = n_heads
    nc = t // chunk
    rb0 = row0 // chunk
    kern = functools.partial(_hgrn_kernel, chunk=chunk, n_valid=n_valid, layer=layer, hp=hp, dk=dk)
    col = lambda off: pl.BlockSpec((chunk, hp * dk),
                                   lambda bi, hi, ci, off=off: (rb0 + bi * nc + ci, (col0 + off) // hp + hi))
    nl = lb_logits.shape[0]
    return pl.pallas_call(
        kern,
        grid=(b, h // hp, nc),
        in_specs=[col(0), col(h), col(2 * h), col(3 * h),
                  pl.BlockSpec((nl, hp * dk), lambda bi, hi, ci: (0, hi)),
                  pl.BlockSpec((1, hp * dv), lambda bi, hi, ci: (0, hi)),
                  pl.BlockSpec((1, hp, dk, dv), lambda bi, hi, ci: (bi, hi, 0, 0))],
        out_specs=[pl.BlockSpec((chunk, hp * dv), lambda bi, hi, ci: (bi * nc + ci, hi)),
                   pl.BlockSpec((1, hp, dk, dv), lambda bi, hi, ci: (bi, hi, 0, 0))],
        out_shape=[jax.ShapeDtypeStruct((b * t, h * dv), BF16),
                   jax.ShapeDtypeStruct((b, h, dk, dv), F32)],
        scratch_shapes=[pltpu.VMEM((hp, dv, dk), F32)],
        compiler_params=_params(("parallel", "parallel", "arbitrary"), 32),
        name="hgrn2",
    )(proj, proj, proj, proj, lb_logits, norm_g.reshape(1, h * dv), s0)


def _rope_tables(pos, half):
    freq = ROPE_BASE ** (-jnp.arange(half, dtype=F32) / half)
    ang = pos.astype(F32)[:, None] * freq[None, :]
    return jnp.cos(ang), jnp.sin(ang)


def _pad_seq(x, b, t, t_pad):
    c = x.shape[1]
    return jnp.pad(x.reshape(b, t, c), ((0, 0), (0, t_pad - t), (0, 0))).reshape(b * t_pad, c)


def _unpad_seq(x, b, t, t_pad):
    return x.reshape(b, t_pad, x.shape[1])[:, :t].reshape(b * t, x.shape[1])


def kernel(x_prompt, x_sample, state_ret, cache_k, cache_v, state_rglru, state_conv, state_hgrn, page_table, norm_ffn1, norm_mix, norm_ffn2, ffn_gate, ffn_up, ffn_down, w_in_ab, w_out_ab, gn_ret, w_in_cd, w_out_cd, conv_w, conv_b, w_rgate, b_rgate, w_igate, b_igate, lru_lambda, hgrn_lb_logits, hgrn_norm, final_norm):
    depth = norm_ffn1.shape[0]
    d_model = x_prompt.shape[-1]
    bp, tp = x_prompt.shape[0], x_prompt.shape[1]
    bs, ts = x_sample.shape[0], x_sample.shape[1]
    mp, ms = bp * tp, bs * ts
    n_ab, _, h_a, dk_a, dv_a = state_ret.shape
    h_b, dh_b = cache_k.shape[3], cache_k.shape[4]
    n_cd, _, w_c = state_rglru.shape
    _, _, h_d, dk_d, dv_d = state_hgrn.shape
    taps = conv_w.shape[1]
    past_len = page_table.shape[1] * cache_k.shape[2]
    a_cols = h_a * dk_a
    b_cols = h_b * dh_b
    b_off = 4 * a_cols // dh_b

    wg, wu, wd = ffn_gate, ffn_up, ffn_down
    win_ab, wout_ab, win_cd, wout_cd = w_in_ab, w_out_ab, w_in_cd, w_out_cd

    ts_ret, ts_lru, ts_hg = 128, 8, 128
    ret_chunk_p = min(tp, 256)
    cos_p, sin_p = _rope_tables(jnp.arange(tp), dk_a // 2)
    cos_s, sin_s = _rope_tables(past_len + jnp.arange(ts_ret), dk_a // 2)
    zeros = lambda *s: jnp.zeros(s, F32)

    def ffn(xp, xs, l, j2, g):
        hp, hs = gate_up(rmsnorm(xp, g, BF16), rmsnorm(xs, g, BF16), wg, wu, l, j2)
        return down_residual(hp, hs, wd, l, j2, xp, xs, 0.5)

    xp = x_prompt.reshape(mp, d_model)
    xs = x_sample.reshape(ms, d_model)
    ret_p, ret_s, k_p, k_s, v_p, v_s = [], [], [], [], [], []
    h_p, h_s, buf_p, buf_s, hg_p, hg_s = [], [], [], [], [], []
    for l in range(depth):
        i = l // 2
        xp, xs = ffn(xp, xs, l, 0, norm_ffn1[l])
        hn_p = rmsnorm(xp, norm_mix[l], BF16)
        hn_s = rmsnorm(xs, norm_mix[l], BF16)
        if l % 2 == 0:
            proj, proj_s = in_proj(hn_p, hn_s, win_ab, i, F32)
            kb_cols = slice(4 * a_cols + b_cols, 4 * a_cols + 2 * b_cols)
            vb_cols = slice(4 * a_cols + 2 * b_cols, 4 * a_cols + 3 * b_cols)
            k_p.append(proj[:, kb_cols].reshape(bp, tp, h_b, dh_b))
            v_p.append(proj[:, vb_cols].reshape(bp, tp, h_b, dh_b))
            k_s.append(proj_s[:, kb_cols].reshape(bs, ts, h_b, dh_b))
            v_s.append(proj_s[:, vb_cols].reshape(bs, ts, h_b, dh_b))

            oa_p, s_p = retention(proj, 0, bp, tp, cos_p, sin_p, gn_ret[i], zeros(bp, h_a, dk_a, dv_a),
                                  n_heads=h_a, dk=dk_a, dv=dv_a, n_valid=ret_chunk_p, chunk=ret_chunk_p)
            oa_s, s_s = retention(_pad_seq(proj_s[:, :4 * a_cols], bs, ts, ts_ret), 0, bs, ts_ret, cos_s, sin_s,
                                  gn_ret[i], state_ret[i], n_heads=h_a, dk=dk_a, dv=dv_a, n_valid=ts, chunk=ts_ret)
            ob_p = moba_prompt(proj, bp, tp, n_heads=h_b, dh=dh_b, q_off=b_off, k_off=b_off + h_b, v_off=b_off + 2 * h_b)
            ob_s = moba_sample(proj_s[:, 4 * a_cols:4 * a_cols + b_cols].reshape(bs, ts, b_cols),
                               proj_s[:, kb_cols].reshape(bs, ts, b_cols), proj_s[:, vb_cols].reshape(bs, ts, b_cols),
                               cache_k, cache_v, i, page_table, n_heads=h_b, dh=dh_b)
            xp, xs = out_proj_residual(oa_p, ob_p, _unpad_seq(oa_s, bs, ts, ts_ret),
                                       ob_s.reshape(ms, b_cols).astype(BF16), wout_ab, i, xp, xs)
            ret_p.append(s_p)
            ret_s.append(s_s)
        else:
            proj, proj_s = in_proj(hn_p, hn_s, win_cd, i, F32)
            xb_p = proj[:, :w_c].reshape(bp, tp, w_c)
            xb_s = proj_s[:, :w_c].reshape(bs, ts, w_c)
            buf_p.append(jnp.concatenate([zeros(bp, taps - 1, w_c), xb_p], axis=1)[:, tp:])
            buf_s.append(jnp.concatenate([state_conv[i], xb_s], axis=1)[:, ts:])

            lru = functools.partial(rglru, conv_w=conv_w[i], conv_b=conv_b[i], w_r=w_rgate[i], b_r=b_rgate[i],
                                    w_i=w_igate[i], b_i=b_igate[i], lam=lru_lambda[i], width=w_c)
            oc_p, hl_p = lru(proj, 0, bp, tp, zeros(bp, taps - 1, w_c), zeros(bp, w_c), n_valid=min(tp, 256), tc=min(tp, 256))
            oc_s, hl_s = lru(_pad_seq(proj_s[:, :2 * w_c], bs, ts, ts_lru), 0, bs, ts_lru, state_conv[i], state_rglru[i],
                             n_valid=ts, tc=ts_lru)
            hg = functools.partial(hgrn2, lb_logits=hgrn_lb_logits, norm_g=hgrn_norm[i], layer=i, n_heads=h_d,
                                   dk=dk_d, dv=dv_d, col0=2 * w_c // dk_d, chunk=128)
            od_p, sg_p = hg(proj, 0, bp, tp, s0=zeros(bp, h_d, dk_d, dv_d), n_valid=128)
            od_s, sg_s = hg(_pad_seq(proj_s, bs, ts, ts_hg), 0, bs, ts_hg, s0=state_hgrn[i], n_valid=ts)
            xp, xs = out_proj_residual(oc_p, od_p, _unpad_seq(oc_s, bs, ts, ts_lru), _unpad_seq(od_s, bs, ts, ts_hg),
                                       wout_cd, i, xp, xs)
            h_p.append(hl_p.reshape(bp, w_c))
            h_s.append(hl_s.reshape(bs, w_c))
            hg_p.append(sg_p)
            hg_s.append(sg_s)
        xp, xs = ffn(xp, xs, l, 1, norm_ffn2[l])
    y_p = rmsnorm(xp, final_norm, F32)
    y_s = rmsnorm(xs, final_norm, F32)
    st = jnp.stack
    return (y_p.reshape(bp, tp, d_model), y_s.reshape(bs, ts, d_model),
            st(ret_p), st(k_p), st(v_p), st(h_p), st(buf_p), st(hg_p),
            st(ret_s), st(k_s), st(v_s), st(h_s), st(buf_s), st(hg_s))
```

```python
import functools

import jax
import jax.numpy as jnp
from jax import lax
from jax.experimental import pallas as pl
from jax.experimental.pallas import tpu as pltpu

F32 = jnp.float32
BF16 = jnp.bfloat16

NORM_EPS = 1e-6
ROPE_BASE = 10000.0
MOBA_BLOCK = 256
MOBA_TOPK = 3
RG_C = 8.0
GATE_BLOCK = 128
SUB = 16
HGRN_HEADS_PER_STEP = 4
MIB = 1024 * 1024
ROW_BLOCK = 1024
COL_BLOCK = 512

_NT = (((1,), (1,)), ((), ()))
_TN = (((0,), (0,)), ((), ()))


def _params(semantics, vmem_mib):
    return pltpu.CompilerParams(dimension_semantics=semantics,
                                vmem_limit_bytes=int(vmem_mib * MIB))


def _bdot(a, b):
    return jnp.dot(a.astype(BF16), b.astype(BF16), preferred_element_type=F32)


def _bdot_nt(a, b):
    return lax.dot_general(a.astype(BF16), b.astype(BF16), _NT, preferred_element_type=F32)


def _bdot_tn(a, b):
    return lax.dot_general(a.astype(BF16), b.astype(BF16), _TN, preferred_element_type=F32)


def _silu(x):
    return x * jax.nn.sigmoid(x)


def _rms_kernel(x_ref, g_ref, o_ref):
    x = x_ref[...]
    ms = jnp.mean(x * x, axis=-1, keepdims=True)
    o_ref[...] = (x * lax.rsqrt(ms + NORM_EPS) * g_ref[...]).astype(o_ref.dtype)


def rmsnorm(x, g, out_dtype):
    m, d = x.shape
    bm = min(m, 256)
    return pl.pallas_call(
        _rms_kernel,
        grid=(pl.cdiv(m, bm),),
        in_specs=[pl.BlockSpec((bm, d), lambda i: (i, 0)),
                  pl.BlockSpec((1, d), lambda i: (0, 0))],
        out_specs=pl.BlockSpec((bm, d), lambda i: (i, 0)),
        out_shape=jax.ShapeDtypeStruct((m, d), out_dtype),
        compiler_params=_params(("parallel",), 32),
        name="rmsnorm",
    )(x, g.reshape(1, d))


def _ahead(j, i, n_j, n_i, lead):
    lead = min(lead, n_i - 1)
    if lead <= 0:
        return j
    return jnp.minimum(j + (i >= n_i - lead).astype(jnp.int32), n_j - 1)


def _round_into(dst, srcs):
    r0 = 0
    for src in srcs:
        dst[r0:r0 + src.shape[0], :] = src[...].astype(BF16)
        r0 += src.shape[0]


def _weight_specs(kparts, bn, lead0, n_j, n_i, prefix, krow0=0):
    specs = []
    for p, kp in enumerate(kparts):
        def imap(j, i, p=p):
            return prefix + (krow0 + p, _ahead(j, i, n_j, n_i, lead0 + p))
        specs.append(pl.BlockSpec((None,) * len(prefix) + (kp, bn), imap))
    return specs


def _mm_kernel(xp_ref, xs_ref, w0_ref, w1_ref, op_ref, os_ref, wb):
    @pl.when(pl.program_id(1) == 0)
    def _():
        _round_into(wb, [w0_ref, w1_ref])
        os_ref[...] = jnp.dot(xs_ref[...], wb[...], preferred_element_type=F32).astype(os_ref.dtype)

    op_ref[...] = jnp.dot(xp_ref[...], wb[...], preferred_element_type=F32).astype(op_ref.dtype)


def in_proj(xp, xs, w_all, idx, out_dtype):
    mp, k = xp.shape
    ms = xs.shape[0]
    n = w_all.shape[-1]
    bm, bn = min(mp, ROW_BLOCK), COL_BLOCK
    n_j, n_i = pl.cdiv(n, bn), mp // bm
    assert mp % bm == 0 and k % 2 == 0
    return pl.pallas_call(
        _mm_kernel,
        grid=(n_j, n_i),
        in_specs=[pl.BlockSpec((bm, k), lambda j, i: (i, 0)),
                  pl.BlockSpec((ms, k), lambda j, i: (0, 0))]
                 + _weight_specs((k // 2, k // 2), bn, 2, n_j, n_i, (idx,)),
        out_specs=[pl.BlockSpec((bm, bn), lambda j, i: (i, j)),
                   pl.BlockSpec((ms, bn), lambda j, i: (0, j))],
        out_shape=[jax.ShapeDtypeStruct((mp, n), out_dtype), jax.ShapeDtypeStruct((ms, n), out_dtype)],
        scratch_shapes=[pltpu.VMEM((k, bn), BF16)],
        compiler_params=_params(("arbitrary", "arbitrary"), 52),
        name="in_proj",
    )(xp, xs, w_all, w_all)


def _gate_up_kernel(xp_ref, xs_ref, wg_ref, wu_ref, op_ref, os_ref, wgb, wub):
    def act(x):
        g = jnp.dot(x, wgb[...], preferred_element_type=F32)
        u = jnp.dot(x, wub[...], preferred_element_type=F32)
        return (_silu(g) * u).astype(op_ref.dtype)

    @pl.when(pl.program_id(1) == 0)
    def _():
        _round_into(wgb, [wg_ref])
        _round_into(wub, [wu_ref])
        os_ref[...] = act(xs_ref[...])

    op_ref[...] = act(xp_ref[...])


def gate_up(xp, xs, wg_all, wu_all, l, j2):
    mp, k = xp.shape
    ms = xs.shape[0]
    n = wg_all.shape[-1]
    bm, bn = min(mp, ROW_BLOCK), COL_BLOCK // 2
    n_j, n_i = pl.cdiv(n, bn), mp // bm
    assert mp % bm == 0
    return pl.pallas_call(
        _gate_up_kernel,
        grid=(n_j, n_i),
        in_specs=[pl.BlockSpec((bm, k), lambda j, i: (i, 0)),
                  pl.BlockSpec((ms, k), lambda j, i: (0, 0))]
                 + _weight_specs((k,), bn, 3, n_j, n_i, (l, j2))
                 + _weight_specs((k,), bn, 2, n_j, n_i, (l, j2)),
        out_specs=[pl.BlockSpec((bm, bn), lambda j, i: (i, j)),
                   pl.BlockSpec((ms, bn), lambda j, i: (0, j))],
        out_shape=[jax.ShapeDtypeStruct((mp, n), BF16), jax.ShapeDtypeStruct((ms, n), BF16)],
        scratch_shapes=[pltpu.VMEM((k, bn), BF16), pltpu.VMEM((k, bn), BF16)],
        compiler_params=_params(("arbitrary", "arbitrary"), 52),
        name="ffn_gate_up",
    )(xp, xs, wg_all, wu_all)


def _down_kernel(hp_ref, hs_ref, w0_ref, w1_ref, xp_ref, xs_ref, op_ref, os_ref, wb, *, scale):
    @pl.when(pl.program_id(1) == 0)
    def _():
        _round_into(wb, [w0_ref, w1_ref])
        os_ref[...] = xs_ref[...] + scale * jnp.dot(hs_ref[...], wb[...], preferred_element_type=F32)

    op_ref[...] = xp_ref[...] + scale * jnp.dot(hp_ref[...], wb[...], preferred_element_type=F32)


def down_residual(hp, hs, w_all, l, j2, xp, xs, scale):
    mp, k = hp.shape
    ms = hs.shape[0]
    n = w_all.shape[-1]
    tk = k // 2
    assert tk * 2 == k and tk % 128 == 0 and (tk // 2) % 8 == 0
    bm, bn = min(mp, ROW_BLOCK // 2), COL_BLOCK
    n_j, n_i = n // bn, mp // bm
    assert mp % bm == 0 and n % bn == 0
    for kb in range(2):
        xp, xs = pl.pallas_call(
            functools.partial(_down_kernel, scale=scale),
            grid=(n_j, n_i),
            in_specs=[pl.BlockSpec((bm, tk), lambda j, i, kb=kb: (i, kb)),
                      pl.BlockSpec((ms, tk), lambda j, i, kb=kb: (0, kb))]
                     + _weight_specs((tk // 2, tk // 2), bn, 2, n_j, n_i, (l, j2), krow0=2 * kb)
                     + [pl.BlockSpec((bm, bn), lambda j, i: (i, j)),
                        pl.BlockSpec((ms, bn), lambda j, i: (0, j))],
            out_specs=[pl.BlockSpec((bm, bn), lambda j, i: (i, j)),
                       pl.BlockSpec((ms, bn), lambda j, i: (0, j))],
            out_shape=[jax.ShapeDtypeStruct((mp, n), F32), jax.ShapeDtypeStruct((ms, n), F32)],
            scratch_shapes=[pltpu.VMEM((tk, bn), BF16)],
            compiler_params=_params(("arbitrary", "arbitrary"), 54),
            name="ffn_down",
        )(hp, hs, w_all, w_all, xp, xs)
    return xp, xs


def _out_proj_kernel(ap_ref, bp_ref, as_ref, bs_ref, wa_ref, wb_ref, xp_ref, xs_ref, op_ref, os_ref, wab, wbb):
    def mix(a, b):
        return (jnp.dot(a, wab[...], preferred_element_type=F32)
                + jnp.dot(b, wbb[...], preferred_element_type=F32))

    @pl.when(pl.program_id(1) == 0)
    def _():
        _round_into(wab, [wa_ref])
        _round_into(wbb, [wb_ref])
        os_ref[...] = xs_ref[...] + mix(as_ref[...], bs_ref[...])

    op_ref[...] = xp_ref[...] + mix(ap_ref[...], bp_ref[...])


def out_proj_residual(ap, bp, as_, bs, w_all, idx, xp, xs):
    mp, ka = ap.shape
    ms = as_.shape[0]
    assert bp.shape[1] == ka and w_all.shape[1] == 2 * ka
    n = w_all.shape[-1]
    bm, bn = min(mp, ROW_BLOCK), COL_BLOCK
    n_j, n_i = n // bn, mp // bm
    assert mp % bm == 0 and n % bn == 0
    row = lambda r, c: pl.BlockSpec((r, c), lambda j, i: (i, 0))
    fix = lambda r, c: pl.BlockSpec((r, c), lambda j, i: (0, 0))
    wa_spec, wb_spec = _weight_specs((ka, ka), bn, 2, n_j, n_i, (idx,))
    return pl.pallas_call(
        _out_proj_kernel,
        grid=(n_j, n_i),
        in_specs=[row(bm, ka), row(bm, ka), fix(ms, ka), fix(ms, ka), wa_spec, wb_spec,
                  pl.BlockSpec((bm, bn), lambda j, i: (i, j)),
                  pl.BlockSpec((ms, bn), lambda j, i: (0, j))],
        out_specs=[pl.BlockSpec((bm, bn), lambda j, i: (i, j)),
                   pl.BlockSpec((ms, bn), lambda j, i: (0, j))],
        out_shape=[jax.ShapeDtypeStruct((mp, n), F32), jax.ShapeDtypeStruct((ms, n), F32)],
        scratch_shapes=[pltpu.VMEM((ka, bn), BF16), pltpu.VMEM((ka, bn), BF16)],
        compiler_params=_params(("arbitrary", "arbitrary"), 54),
        name="out_proj",
    )(ap, bp, as_, bs, w_all, w_all, xp, xs)


def _cumsum_rows(x):
    n = x.shape[0]
    row = lax.broadcasted_iota(jnp.int32, x.shape, 0)
    s = 1
    while s < n:
        x = x + jnp.where(row >= s, pltpu.roll(x, s, 0), 0.0)
        s *= 2
    return x


def _linear_scan_rows(a, u):
    n = a.shape[0]
    row = lax.broadcasted_iota(jnp.int32, a.shape, 0)
    s = 1
    while s < n:
        keep = row >= s
        a_prev = jnp.where(keep, pltpu.roll(a, s, 0), 1.0)
        u_prev = jnp.where(keep, pltpu.roll(u, s, 0), 0.0)
        u = a * u_prev + u
        a = a * a_prev
        s *= 2
    return a, u


def _retention_kernel(q_ref, k_ref, v_ref, g_ref, cos_ref, sin_ref, lg_ref, gn_ref, s0_ref,
                      o_ref, sout_ref, s_scr, dec_scr, *, chunk, n_valid, dk):
    c = pl.program_id(2)
    log_g = lg_ref[0][:, :1]

    @pl.when(c == 0)
    def _():
        s_scr[...] = s0_ref[0, 0]
        i = lax.broadcasted_iota(jnp.int32, (chunk, chunk), 0)
        j = lax.broadcasted_iota(jnp.int32, (chunk, chunk), 1)
        d = (i - j).astype(F32) * log_g
        dec_scr[...] = jnp.exp(jnp.where(i >= j, d, -jnp.inf))

    half = dk // 2
    cos = cos_ref[...]
    sin = sin_ref[...]

    def rot(x):
        x1 = x[:, :half]
        x2 = x[:, half:]
        return jnp.concatenate([x1 * cos - x2 * sin, x1 * sin + x2 * cos], axis=-1)

    q = rot(q_ref[...])
    k = rot(k_ref[...]) * (dk ** -0.5)
    v = v_ref[...]
    idx = lax.broadcasted_iota(jnp.int32, (chunk, 1), 0).astype(F32)
    inter = jnp.exp((idx + 1.0) * log_g)
    to_state = jnp.exp((n_valid - 1.0 - idx) * log_g)
    carry = jnp.exp(n_valid * log_g)

    s = s_scr[...]
    att = _bdot_nt(q, k) * dec_scr[...]
    o = _bdot(att, v) + _bdot(q, s) * inter
    s_new = carry * s + _bdot_tn(k * to_state, v)
    s_scr[...] = s_new

    mu = jnp.mean(o, axis=-1, keepdims=True)
    oc = o - mu
    var = jnp.mean(oc * oc, axis=-1, keepdims=True)
    y = oc * lax.rsqrt(var + NORM_EPS) * gn_ref[...]
    o_ref[...] = (y * _silu(g_ref[...])).astype(o_ref.dtype)

    @pl.when(c == pl.num_programs(2) - 1)
    def _():
        sout_ref[0, 0] = s_new


def retention(proj, row0, b, t, cos, sin, gn, s0, *, n_heads, dk, dv, n_valid, chunk):
    assert dk == dv and t % chunk == 0 and row0 % chunk == 0 and (n_valid == chunk or t == chunk)
    h = n_heads
    nc = t // chunk
    rb0 = row0 // chunk
    log_g = jnp.log1p(-jnp.exp2(-5.0 - jnp.arange(h, dtype=F32)))
    lg = jnp.broadcast_to(log_g[:, None, None], (h, 1, 128))
    kern = functools.partial(_retention_kernel, chunk=chunk, n_valid=n_valid, dk=dk)
    col = lambda off: pl.BlockSpec((chunk, dk), lambda bi, hi, ci, off=off: (rb0 + bi * nc + ci, off + hi))
    return pl.pallas_call(
        kern,
        grid=(b, h, nc),
        in_specs=[col(0), col(h), col(2 * h), col(3 * h),
                  pl.BlockSpec((chunk, dk // 2), lambda bi, hi, ci: (ci, 0)),
                  pl.BlockSpec((chunk, dk // 2), lambda bi, hi, ci: (ci, 0)),
                  pl.BlockSpec((1, 1, 128), lambda bi, hi, ci: (hi, 0, 0)),
                  pl.BlockSpec((1, dv), lambda bi, hi, ci: (0, hi)),
                  pl.BlockSpec((1, 1, dk, dv), lambda bi, hi, ci: (bi, hi, 0, 0))],
        out_specs=[pl.BlockSpec((chunk, dv), lambda bi, hi, ci: (bi * nc + ci, hi)),
                   pl.BlockSpec((1, 1, dk, dv), lambda bi, hi, ci: (bi, hi, 0, 0))],
        out_shape=[jax.ShapeDtypeStruct((b * t, h * dv), BF16),
                   jax.ShapeDtypeStruct((b, h, dk, dv), F32)],
        scratch_shapes=[pltpu.VMEM((dk, dv), F32), pltpu.VMEM((chunk, chunk), F32)],
        compiler_params=_params(("parallel", "parallel", "arbitrary"), 32),
        name="retention",
    )(proj, proj, proj, proj, cos, sin, lg, gn.reshape(1, h * dv), s0)


def _moba_prompt_kernel(q_ref, k_ref, v_ref, o_ref, sel_scr, kb_scr, vb_scr, *, nb, scale):
    i = pl.program_id(2)
    blk = MOBA_BLOCK

    @pl.when(i == 0)
    def _():
        kk = k_ref[...]
        t = kk.shape[0]
        kb_scr[...] = kk.astype(BF16)
        vb_scr[...] = v_ref[...].astype(BF16)
        kmean = jnp.mean(kk.reshape(nb, blk, kk.shape[-1]), axis=1)
        gate = lax.dot_general(kmean, q_ref[...], _NT, precision=lax.Precision.HIGHEST,
                               preferred_element_type=F32)
        bidx = lax.broadcasted_iota(jnp.int32, (nb, t), 0)
        past = bidx < lax.broadcasted_iota(jnp.int32, (nb, t), 1) // blk
        gate = jnp.where(past, gate, -jnp.inf)
        rank = jnp.zeros(gate.shape, jnp.int32)
        for mm in range(nb):
            gm = gate[mm:mm + 1, :]
            rank = rank + ((gm > gate) | ((gm == gate) & (bidx > mm))).astype(jnp.int32)
        sel_scr[...] = jnp.where((rank < MOBA_TOPK) & past, 1.0, 0.0).T

    own = pl.multiple_of(i * blk, blk)
    qb = q_ref[pl.ds(own, blk), :].astype(BF16)
    sel = sel_scr[pl.ds(own, blk), :]

    row = lax.broadcasted_iota(jnp.int32, (blk, blk), 0)
    colk = lax.broadcasted_iota(jnp.int32, (blk, blk), 1)
    causal = colk <= row

    def attend(nkb):
        s = _bdot_nt(qb, kb_scr[:nkb * blk, :]) * scale
        pieces = []
        for n in range(nkb):
            own = jnp.full((blk, blk), n, jnp.int32) == i
            allowed = (jnp.broadcast_to(sel[:, n:n + 1], (blk, blk)) > 0.5) | (own & causal)
            pieces.append(jnp.where(allowed, s[:, n * blk:(n + 1) * blk], -jnp.inf))
        s = jnp.concatenate(pieces, axis=-1)
        m = jnp.max(s, axis=-1, keepdims=True)
        p = jnp.exp(s - m)
        l = jnp.sum(p, axis=-1, keepdims=True)
        o_ref[...] = (_bdot(p, vb_scr[:nkb * blk, :]) / l).astype(o_ref.dtype)

    step = 2 if nb % 2 == 0 else 1
    for nkb in range(step, nb + 1, step):
        @pl.when((i >= nkb - step) & (i < nkb))
        def _(nkb=nkb):
            attend(nkb)


def moba_prompt(proj, b, t, *, n_heads, dh, q_off, k_off, v_off):
    assert t % MOBA_BLOCK == 0
    nb = t // MOBA_BLOCK
    kern = functools.partial(_moba_prompt_kernel, nb=nb, scale=dh ** -0.5)
    return pl.pallas_call(
        kern,
        grid=(b, n_heads, nb),
        in_specs=[pl.BlockSpec((t, dh), lambda bi, hi, ii: (bi, q_off + hi)),
                  pl.BlockSpec((t, dh), lambda bi, hi, ii: (bi, k_off + hi)),
                  pl.BlockSpec((t, dh), lambda bi, hi, ii: (bi, v_off + hi))],
        out_specs=pl.BlockSpec((MOBA_BLOCK, dh), lambda bi, hi, ii: (bi * nb + ii, hi)),
        out_shape=jax.ShapeDtypeStruct((b * t, n_heads * dh), BF16),
        scratch_shapes=[pltpu.VMEM((t, nb), F32), pltpu.VMEM((t, dh), BF16), pltpu.VMEM((t, dh), BF16)],
        compiler_params=_params(("parallel", "parallel", "arbitrary"), 40),
        name="moba_prompt",
    )(proj, proj, proj)


def _moba_decode_kernel(pt_ref, q_ref, knew_ref, vnew_ref, *refs, n_heads, nblk, ts, ppb, scale):
    k_refs, v_refs = refs[:ppb], refs[ppb:2 * ppb]
    o_ref, km_scr, bias_scr, mb_scr, lb_scr, accb_scr = refs[2 * ppb:]
    n = pl.program_id(1)
    q = q_ref[0]
    rows, dh = q.shape
    qb = q.astype(BF16)

    @pl.when(n == 0)
    def _():
        r = lax.broadcasted_iota(jnp.int32, bias_scr.shape, 0)
        c = lax.broadcasted_iota(jnp.int32, bias_scr.shape, 1)
        bias_scr[...] = jnp.where((r % n_heads) == (c % n_heads), 0.0, -jnp.inf)

    ksum = jnp.sum(k_refs[0][...], axis=0)
    for k_ref in k_refs[1:]:
        ksum = ksum + jnp.sum(k_ref[...], axis=0)
    km_scr[n] = ksum * (1.0 / MOBA_BLOCK)

    bias = bias_scr[...]
    scores = [_bdot_nt(qb, k_ref[...].reshape(-1, dh)) * scale + bias for k_ref in k_refs]
    m_n = jnp.max(scores[0], axis=-1, keepdims=True)
    for s in scores[1:]:
        m_n = jnp.maximum(m_n, jnp.max(s, axis=-1, keepdims=True))
    l_n = jnp.zeros((rows, 1), F32)
    acc_n = jnp.zeros((rows, dh), F32)
    for s, v_ref in zip(scores, v_refs):
        e = jnp.exp(s - m_n)
        l_n = l_n + jnp.sum(e, axis=-1, keepdims=True)
        acc_n = acc_n + _bdot(e, v_ref[...].reshape(-1, dh))
    mb_scr[n] = m_n
    lb_scr[n] = l_n
    accb_scr[n] = acc_n

    @pl.when(n == nblk - 1)
    def _():
        km = km_scr[...].reshape(nblk * n_heads, dh)
        g = lax.dot_general(q, km, _NT, precision=lax.Precision.HIGHEST, preferred_element_type=F32)
        width = nblk * n_heads
        rs = lax.broadcasted_iota(jnp.int32, (rows, width), 0)
        cs = lax.broadcasted_iota(jnp.int32, (rows, width), 1)
        nidx = cs // n_heads
        rank = jnp.zeros((rows, width), jnp.int32)
        for k in range(1, nblk):
            other = pltpu.roll(g, k * n_heads, 1)
            rank = rank + ((other > g) | ((other == g) & (nidx >= k))).astype(jnp.int32)
        sel = jnp.where((rank < MOBA_TOPK) & ((cs % n_heads) == (rs % n_heads)), 1.0, 0.0)
        flags = [jnp.sum(sel[:, blk * n_heads:(blk + 1) * n_heads], axis=-1, keepdims=True) > 0.5
                 for blk in range(nblk)]

        kn = knew_ref[0].reshape(-1, dh)
        vn = vnew_ref[0].reshape(-1, dh)
        r = lax.broadcasted_iota(jnp.int32, (rows, kn.shape[0]), 0)
        c = lax.broadcasted_iota(jnp.int32, (rows, kn.shape[0]), 1)
        tj = c // n_heads
        ok = ((r % n_heads) == (c % n_heads)) & (tj <= r // n_heads) & (tj < ts)
        s = jnp.where(ok, _bdot_nt(qb, kn) * scale, -jnp.inf)
        m_own = jnp.max(s, axis=-1, keepdims=True)

        m = m_own
        for blk in range(nblk):
            m = jnp.maximum(m, jnp.where(flags[blk], mb_scr[blk], -jnp.inf))
        e = jnp.exp(s - m)
        l = jnp.sum(e, axis=-1, keepdims=True)
        acc = _bdot(e, vn)
        for blk in range(nblk):
            w = jnp.where(flags[blk], jnp.exp(mb_scr[blk] - m), 0.0)
            l = l + w * lb_scr[blk]
            acc = acc + w * accb_scr[blk]
        o_ref[0] = acc / l


def moba_sample(qb, knew, vnew, cache_k, cache_v, layer, page_table, *, n_heads, dh):
    b, ts, w = qb.shape
    n_pages = page_table.shape[1]
    page = cache_k.shape[2]
    past = n_pages * page
    assert past % MOBA_BLOCK == 0 and MOBA_BLOCK % page == 0 and ts <= MOBA_BLOCK and w == n_heads * dh
    ppb = MOBA_BLOCK // page
    nblk = past // MOBA_BLOCK
    rows = ts * n_heads
    ts_pad = -(-ts // 8) * 8
    pad = ((0, 0), (0, ts_pad - ts), (0, 0), (0, 0))
    q3 = qb.reshape(b, rows, dh)
    kn = jnp.pad(knew.reshape(b, ts, n_heads, dh), pad)
    vn = jnp.pad(vnew.reshape(b, ts, n_heads, dh), pad)
    page_specs = [pl.BlockSpec((None, None, page, n_heads, dh),
                               lambda bi, ni, pt, pg=pg: (layer, pt[bi, ni * ppb + pg], 0, 0, 0))
                  for pg in range(ppb)]

    out = pl.pallas_call(
        functools.partial(_moba_decode_kernel, n_heads=n_heads, nblk=nblk, ts=ts, ppb=ppb, scale=dh ** -0.5),
        grid_spec=pltpu.PrefetchScalarGridSpec(
            num_scalar_prefetch=1,
            grid=(b, nblk),
            in_specs=[pl.BlockSpec((1, rows, dh), lambda bi, ni, pt: (bi, 0, 0)),
                      pl.BlockSpec((1, ts_pad, n_heads, dh), lambda bi, ni, pt: (bi, 0, 0, 0)),
                      pl.BlockSpec((1, ts_pad, n_heads, dh), lambda bi, ni, pt: (bi, 0, 0, 0))]
                     + page_specs + page_specs,
            out_specs=pl.BlockSpec((1, rows, dh), lambda bi, ni, pt: (bi, 0, 0)),
            scratch_shapes=[pltpu.VMEM((nblk, n_heads, dh), F32), pltpu.VMEM((rows, page * n_heads), F32),
                            pltpu.VMEM((nblk, rows, 1), F32), pltpu.VMEM((nblk, rows, 1), F32),
                            pltpu.VMEM((nblk, rows, dh), F32)]),
        out_shape=jax.ShapeDtypeStruct((b, rows, dh), F32),
        compiler_params=_params(("parallel", "arbitrary"), 40),
        name="moba_sample_attn",
    )(page_table, q3, kn, vn, *([cache_k] * ppb), *([cache_v] * ppb))
    return out.reshape(b, ts, w)


def _rglru_kernel(x_ref, g_ref, tail0_ref, h0_ref, cw_ref, cb_ref, wr_ref, br_ref, wi_ref, bi_ref, lam_ref,
                  o_ref, hlast_ref, tail_scr, h_scr, *, tc, n_valid):
    c = pl.program_id(2)

    @pl.when(c == 0)
    def _():
        tail_scr[...] = tail0_ref[0]
        h_scr[...] = h0_ref[0]

    x = x_ref[...]
    cw = cw_ref[...]
    taps = cw.shape[0]
    ext = jnp.concatenate([tail_scr[...], x], axis=0)
    xc = cb_ref[...] + x * cw[taps - 1:taps, :]
    for s in range(1, taps):
        xc = xc + pltpu.roll(ext, s, 0)[8:, :] * cw[taps - 1 - s:taps - s, :]
    tail_scr[...] = x[tc - 8:, :]

    nblk = x.shape[1] // GATE_BLOCK
    zr, zi = [], []
    for kb in range(nblk):
        xb = xc[:, kb * GATE_BLOCK:(kb + 1) * GATE_BLOCK]
        zr.append(_bdot(xb, wr_ref[kb]))
        zi.append(_bdot(xb, wi_ref[kb]))
    r = jax.nn.sigmoid(jnp.concatenate(zr, axis=-1) + br_ref[...])
    ig = jax.nn.sigmoid(jnp.concatenate(zi, axis=-1) + bi_ref[...])
    lam = lam_ref[...]
    softplus_neg = jnp.maximum(-lam, 0.0) + jnp.log1p(jnp.exp(-jnp.abs(lam)))
    log_a = -RG_C * r * softplus_neg
    a = jnp.exp(log_a)
    u = jnp.sqrt(jnp.tanh(-log_a) * (1.0 + a * a)) * ig * xc
    a_cum, hz = _linear_scan_rows(a, u)
    h = hz + a_cum * h_scr[...]
    h_scr[...] = h[n_valid - 1:n_valid, :]
    o_ref[...] = (h * jax.nn.gelu(g_ref[...])).astype(o_ref.dtype)

    @pl.when(c == pl.num_programs(2) - 1)
    def _():
        hlast_ref[0] = h[n_valid - 1:n_valid, :]


def rglru(proj, row0, b, t, conv_buf, h0, conv_w, conv_b, w_r, b_r, w_i, b_i, lam, *, width, n_valid, tc, bw=512):
    taps = conv_w.shape[0]
    assert t % tc == 0 and tc % 8 == 0 and row0 % tc == 0 and (n_valid == tc or t == tc) and taps - 1 <= 8
    nw = width // bw
    gpb = bw // GATE_BLOCK
    nc = t // tc
    rb0 = row0 // tc
    tail0 = jnp.pad(conv_buf, ((0, 0), (8 - (taps - 1), 0), (0, 0)))
    vec = lambda a: a.reshape(1, width)
    vspec = pl.BlockSpec((1, bw), lambda bi, wi, ci: (0, wi))
    kern = functools.partial(_rglru_kernel, tc=tc, n_valid=n_valid)
    return pl.pallas_call(
        kern,
        grid=(b, nw, nc),
        in_specs=[pl.BlockSpec((tc, bw), lambda bi, wi, ci: (rb0 + bi * nc + ci, wi)),
                  pl.BlockSpec((tc, bw), lambda bi, wi, ci: (rb0 + bi * nc + ci, nw + wi)),
                  pl.BlockSpec((1, 8, bw), lambda bi, wi, ci: (bi, 0, wi)),
                  pl.BlockSpec((1, 1, bw), lambda bi, wi, ci: (bi, 0, wi)),
                  pl.BlockSpec((taps, bw), lambda bi, wi, ci: (0, wi)),
                  vspec,
                  pl.BlockSpec((gpb, GATE_BLOCK, GATE_BLOCK), lambda bi, wi, ci: (wi, 0, 0)),
                  vspec,
                  pl.BlockSpec((gpb, GATE_BLOCK, GATE_BLOCK), lambda bi, wi, ci: (wi, 0, 0)),
                  vspec, vspec],
        out_specs=[pl.BlockSpec((tc, bw), lambda bi, wi, ci: (bi * nc + ci, wi)),
                   pl.BlockSpec((1, 1, bw), lambda bi, wi, ci: (bi, 0, wi))],
        out_shape=[jax.ShapeDtypeStruct((b * t, width), BF16),
                   jax.ShapeDtypeStruct((b, 1, width), F32)],
        scratch_shapes=[pltpu.VMEM((8, bw), F32), pltpu.VMEM((1, bw), F32)],
        compiler_params=_params(("parallel", "parallel", "arbitrary"), 32),
        name="rglru",
    )(proj, proj, tail0, h0.reshape(b, 1, width), conv_w, vec(conv_b), w_r, vec(b_r), w_i, vec(b_i), vec(lam))


def _hgrn_head(q, fz, v, g, lb, ng, st, *, chunk, n_valid):
    dk = q.shape[1]
    log_f = jnp.log(lb + (1.0 - lb) * jax.nn.sigmoid(fz))
    key = (1.0 - lb) * jax.nn.sigmoid(-fz)
    if n_valid < chunk:
        live = lax.broadcasted_iota(jnp.int32, (chunk, dk), 0) < n_valid
        log_f = jnp.where(live, log_f, 0.0)
        key = jnp.where(live, key, 0.0)
    cum = _cumsum_rows(log_f)
    last = cum[chunk - 1:chunk, :]

    o = _bdot_nt(q * jnp.exp(cum), st)

    row = lax.broadcasted_iota(jnp.int32, (chunk, dk), 0)
    ri = lax.broadcasted_iota(jnp.int32, (chunk, chunk), 0)
    ci = lax.broadcasted_iota(jnp.int32, (chunk, chunk), 1)
    att = None
    s = chunk // 2
    while s >= SUB:
        grp = chunk // (2 * s)
        edge = jnp.broadcast_to(cum.reshape(grp, 2 * s, dk)[:, s - 1:s, :], (grp, 2 * s, dk)).reshape(chunk, dk)
        upper = (row % (2 * s)) >= s
        qs = q * jnp.exp(jnp.where(upper, cum - edge, -jnp.inf))
        ks = key * jnp.exp(jnp.where(upper, -jnp.inf, edge - cum))
        a = _bdot_nt(qs, ks)
        if grp > 1:
            a = jnp.where((ri // (2 * s)) == (ci // (2 * s)), a, 0.0)
        att = a if att is None else att + a
        s //= 2
    if att is not None:
        o = o + _bdot(att, v)

    nsb = chunk // SUB
    q3 = q.reshape(nsb, SUB, dk)
    k3 = key.reshape(nsb, SUB, dk)
    c3 = cum.reshape(nsb, SUB, dk)
    v3 = v.reshape(nsb, SUB, v.shape[1])
    ri = lax.broadcasted_iota(jnp.int32, (nsb, SUB, dk), 1)
    od = jnp.zeros(v3.shape, F32)
    for j in range(SUB):
        d = jnp.where(ri >= j, c3 - c3[:, j:j + 1, :], -jnp.inf)
        wgt = jnp.sum(q3 * jnp.exp(d) * k3[:, j:j + 1, :], axis=-1, keepdims=True)
        od = od + wgt * v3[:, j:j + 1, :]
    o = o + od.reshape(chunk, v.shape[1])

    st_new = st * jnp.exp(last) + _bdot_tn(v, key * jnp.exp(last - cum))
    y = o * lax.rsqrt(jnp.mean(o * o, axis=-1, keepdims=True) + NORM_EPS) * ng
    return y * _silu(g), st_new


def _hgrn_kernel(q_ref, f_ref, i_ref, g_ref, lbl_ref, ng_ref, s0_ref, o_ref, sout_ref, st_scr,
                 *, chunk, n_valid, layer, hp, dk):
    c = pl.program_id(2)

    @pl.when(c == 0)
    def _():
        for k in range(hp):
            st_scr[k] = s0_ref[0, k].T

    lbl = lbl_ref[...]
    e = jnp.exp(lbl - jnp.max(lbl, axis=0, keepdims=True))
    soft = e / jnp.sum(e, axis=0, keepdims=True)
    lb_all = jnp.zeros((1, lbl.shape[1]), F32)
    for r in range(1, layer + 1):
        lb_all = lb_all + soft[r:r + 1, :]

    outs, states = [], []
    for k in range(hp):
        sl = slice(k * dk, (k + 1) * dk)
        y, st_new = _hgrn_head(q_ref[:, sl], f_ref[:, sl], i_ref[:, sl], g_ref[:, sl], lb_all[:, sl],
                               ng_ref[:, sl], st_scr[k], chunk=chunk, n_valid=n_valid)
        st_scr[k] = st_new
        outs.append(y)
        states.append(st_new)
    o_ref[...] = jnp.concatenate(outs, axis=-1).astype(o_ref.dtype)

    @pl.when(c == pl.num_programs(2) - 1)
    def _():
        for k in range(hp):
            sout_ref[0, k] = states[k].T


def hgrn2(proj, row0, b, t, lb_logits, norm_g, s0, *, layer, n_heads, dk, dv, col0, n_valid, chunk=128):
    hp = HGRN_HEADS_PER_STEP
    assert dk == dv and t % chunk == 0 and chunk % SUB == 0 and row0 % chunk == 0 and (n_valid == chunk or t == chunk)
    assert n_heads % hp == 0 and col0 % hp == 0
    h = n_heads
    nc = t // chunk
    rb0 = row0 // chunk
    kern = functools.partial(_hgrn_kernel, chunk=chunk, n_valid=n_valid, layer=layer, hp=hp, dk=dk)
    col = lambda off: pl.BlockSpec((chunk, hp * dk),
                                   lambda bi, hi, ci, off=off: (rb0 + bi * nc + ci, (col0 + off) // hp + hi))
    nl = lb_logits.shape[0]
    return pl.pallas_call(
        kern,
        grid=(b, h // hp, nc),
        in_specs=[col(0), col(h), col(2 * h), col(3 * h),
                  pl.BlockSpec((nl, hp * dk), lambda bi, hi, ci: (0, hi)),
                  pl.BlockSpec((1, hp * dv), lambda bi, hi, ci: (0, hi)),
                  pl.BlockSpec((1, hp, dk, dv), lambda bi, hi, ci: (bi, hi, 0, 0))],
        out_specs=[pl.BlockSpec((chunk, hp * dv), lambda bi, hi, ci: (bi * nc + ci, hi)),
                   pl.BlockSpec((1, hp, dk, dv), lambda bi, hi, ci: (bi, hi, 0, 0))],
        out_shape=[jax.ShapeDtypeStruct((b * t, h * dv), BF16),
                   jax.ShapeDtypeStruct((b, h, dk, dv), F32)],
        scratch_shapes=[pltpu.VMEM((hp, dv, dk), F32)],
        compiler_params=_params(("parallel", "parallel", "arbitrary"), 32),
        name="hgrn2",
    )(proj, proj, proj, proj, lb_logits, norm_g.reshape(1, h * dv), s0)


def _rope_tables(pos, half):
    freq = ROPE_BASE ** (-jnp.arange(half, dtype=F32) / half)
    ang = pos.astype(F32)[:, None] * freq[None, :]
    return jnp.cos(ang), jnp.sin(ang)


def _pad_seq(x, b, t, t_pad):
    c = x.shape[1]
    return jnp.pad(x.reshape(b, t, c), ((0, 0), (0, t_pad - t), (0, 0))).reshape(b * t_pad, c)


def _unpad_seq(x, b, t, t_pad):
    return x.reshape(b, t_pad, x.shape[1])[:, :t].reshape(b * t, x.shape[1])


def kernel(x_prompt, x_sample, state_ret, cache_k, cache_v, state_rglru, state_conv, state_hgrn, page_table, norm_ffn1, norm_mix, norm_ffn2, ffn_gate, ffn_up, ffn_down, w_in_ab, w_out_ab, gn_ret, w_in_cd, w_out_cd, conv_w, conv_b, w_rgate, b_rgate, w_igate, b_igate, lru_lambda, hgrn_lb_logits, hgrn_norm, final_norm):
    depth = norm_ffn1.shape[0]
    d_model = x_prompt.shape[-1]
    bp, tp = x_prompt.shape[0], x_prompt.shape[1]
    bs, ts = x_sample.shape[0], x_sample.shape[1]
    mp, ms = bp * tp, bs * ts
    n_ab, _, h_a, dk_a, dv_a = state_ret.shape
    h_b, dh_b = cache_k.shape[3], cache_k.shape[4]
    n_cd, _, w_c = state_rglru.shape
    _, _, h_d, dk_d, dv_d = state_hgrn.shape
    taps = conv_w.shape[1]
    past_len = page_table.shape[1] * cache_k.shape[2]
    a_cols = h_a * dk_a
    b_cols = h_b * dh_b
    b_off = 4 * a_cols // dh_b

    wg, wu, wd = ffn_gate, ffn_up, ffn_down
    win_ab, wout_ab, win_cd, wout_cd = w_in_ab, w_out_ab, w_in_cd, w_out_cd

    ts_ret, ts_lru, ts_hg = 128, 8, 128
    ret_chunk_p = min(tp, 256)
    cos_p, sin_p = _rope_tables(jnp.arange(tp), dk_a // 2)
    cos_s, sin_s = _rope_tables(past_len + jnp.arange(ts_ret), dk_a // 2)
    zeros = lambda *s: jnp.zeros(s, F32)

    def ffn(xp, xs, l, j2, g):
        hp, hs = gate_up(rmsnorm(xp, g, BF16), rmsnorm(xs, g, BF16), wg, wu, l, j2)
        return down_residual(hp, hs, wd, l, j2, xp, xs, 0.5)

    xp = x_prompt.reshape(mp, d_model)
    xs = x_sample.reshape(ms, d_model)
    ret_p, ret_s, k_p, k_s, v_p, v_s = [], [], [], [], [], []
    h_p, h_s, buf_p, buf_s, hg_p, hg_s = [], [], [], [], [], []
    for l in range(depth):
        i = l // 2
        xp, xs = ffn(xp, xs, l, 0, norm_ffn1[l])
        hn_p = rmsnorm(xp, norm_mix[l], BF16)
        hn_s = rmsnorm(xs, norm_mix[l], BF16)
        if l % 2 == 0:
            proj, proj_s = in_proj(hn_p, hn_s, win_ab, i, F32)
            kb_cols = slice(4 * a_cols + b_cols, 4 * a_cols + 2 * b_cols)
            vb_cols = slice(4 * a_cols + 2 * b_cols, 4 * a_cols + 3 * b_cols)
            k_p.append(proj[:, kb_cols].reshape(bp, tp, h_b, dh_b))
            v_p.append(proj[:, vb_cols].reshape(bp, tp, h_b, dh_b))
            k_s.append(proj_s[:, kb_cols].reshape(bs, ts, h_b, dh_b))
            v_s.append(proj_s[:, vb_cols].reshape(bs, ts, h_b, dh_b))

            oa_p, s_p = retention(proj, 0, bp, tp, cos_p, sin_p, gn_ret[i], zeros(bp, h_a, dk_a, dv_a),
                                  n_heads=h_a, dk=dk_a, dv=dv_a, n_valid=ret_chunk_p, chunk=ret_chunk_p)
            oa_s, s_s = retention(_pad_seq(proj_s[:, :4 * a_cols], bs, ts, ts_ret), 0, bs, ts_ret, cos_s, sin_s,
                                  gn_ret[i], state_ret[i], n_heads=h_a, dk=dk_a, dv=dv_a, n_valid=ts, chunk=ts_ret)
            ob_p = moba_prompt(proj, bp, tp, n_heads=h_b, dh=dh_b, q_off=b_off, k_off=b_off + h_b, v_off=b_off + 2 * h_b)
            ob_s = moba_sample(proj_s[:, 4 * a_cols:4 * a_cols + b_cols].reshape(bs, ts, b_cols),
                               proj_s[:, kb_cols].reshape(bs, ts, b_cols), proj_s[:, vb_cols].reshape(bs, ts, b_cols),
                               cache_k, cache_v, i, page_table, n_heads=h_b, dh=dh_b)
            xp, xs = out_proj_residual(oa_p, ob_p, _unpad_seq(oa_s, bs, ts, ts_ret),
                                       ob_s.reshape(ms, b_cols).astype(BF16), wout_ab, i, xp, xs)
            ret_p.append(s_p)
            ret_s.append(s_s)
        else:
            proj, proj_s = in_proj(hn_p, hn_s, win_cd, i, F32)
            xb_p = proj[:, :w_c].reshape(bp, tp, w_c)
            xb_s = proj_s[:, :w_c].reshape(bs, ts, w_c)
            buf_p.append(jnp.concatenate([zeros(bp, taps - 1, w_c), xb_p], axis=1)[:, tp:])
            buf_s.append(jnp.concatenate([state_conv[i], xb_s], axis=1)[:, ts:])

            lru = functools.partial(rglru, conv_w=conv_w[i], conv_b=conv_b[i], w_r=w_rgate[i], b_r=b_rgate[i],
                                    w_i=w_igate[i], b_i=b_igate[i], lam=lru_lambda[i], width=w_c)
            oc_p, hl_p = lru(proj, 0, bp, tp, zeros(bp, taps - 1, w_c), zeros(bp, w_c), n_valid=min(tp, 256), tc=min(tp, 256))
            oc_s, hl_s = lru(_pad_seq(proj_s[:, :2 * w_c], bs, ts, ts_lru), 0, bs, ts_lru, state_conv[i], state_rglru[i],
                             n_valid=ts, tc=ts_lru)
            hg = functools.partial(hgrn2, lb_logits=hgrn_lb_logits, norm_g=hgrn_norm[i], layer=i, n_heads=h_d,
                                   dk=dk_d, dv=dv_d, col0=2 * w_c // dk_d, chunk=128)
            od_p, sg_p = hg(proj, 0, bp, tp, s0=zeros(bp, h_d, dk_d, dv_d), n_valid=128)
            od_s, sg_s = hg(_pad_seq(proj_s, bs, ts, ts_hg), 0, bs, ts_hg, s0=state_hgrn[i], n_valid=ts)
            xp, xs = out_proj_residual(oc_p, od_p, _unpad_seq(oc_s, bs, ts, ts_lru), _unpad_seq(od_s, bs, ts, ts_hg),
                                       wout_cd, i, xp, xs)
            h_p.append(hl_p.reshape(bp, w_c))
            h_s.append(hl_s.reshape(bs, w_c))
            hg_p.append(sg_p)
            hg_s.append(sg_s)
        xp, xs = ffn(xp, xs, l, 1, norm_ffn2[l])
    y_p = rmsnorm(xp, final_norm, F32)
    y_s = rmsnorm(xs, final_norm, F32)
    st = jnp.stack
    return (y_p.reshape(bp, tp, d_model), y_s.reshape(bs, ts, d_model),
            st(ret_p), st(k_p), st(v_p), st(h_p), st(buf_p), st(hg_p),
            st(ret_s), st(k_s), st(v_s), st(h_s), st(buf_s), st(hg_s))
```

```python
import functools

import jax
import jax.numpy as jnp
from jax import lax
from jax.experimental import pallas as pl
from jax.experimental.pallas import tpu as pltpu

F32 = jnp.float32
BF16 = jnp.bfloat16

NORM_EPS = 1e-6
ROPE_BASE = 10000.0
MOBA_BLOCK = 256
MOBA_TOPK = 3
RG_C = 8.0
GATE_BLOCK = 128
SUB = 8
HGRN_HEADS_PER_STEP = 4
MOBA_HEADS_PER_STEP = 2
MIB = 1024 * 1024
ROW_BLOCK = 1024
COL_BLOCK = 512

_NT = (((1,), (1,)), ((), ()))
_TN = (((0,), (0,)), ((), ()))


def _params(semantics, vmem_mib):
    return pltpu.CompilerParams(dimension_semantics=semantics,
                                vmem_limit_bytes=int(vmem_mib * MIB))


def _bdot(a, b):
    return jnp.dot(a.astype(BF16), b.astype(BF16), preferred_element_type=F32)


def _bdot_nt(a, b):
    return lax.dot_general(a.astype(BF16), b.astype(BF16), _NT, preferred_element_type=F32)


def _bdot_tn(a, b):
    return lax.dot_general(a.astype(BF16), b.astype(BF16), _TN, preferred_element_type=F32)


def _silu(x):
    return x * jax.nn.sigmoid(x)


def _rms_kernel(x_ref, g_ref, o_ref):
    x = x_ref[...]
    ms = jnp.mean(x * x, axis=-1, keepdims=True)
    o_ref[...] = (x * lax.rsqrt(ms + NORM_EPS) * g_ref[...]).astype(o_ref.dtype)


def rmsnorm(x, g, out_dtype):
    m, d = x.shape
    bm = min(m, 256)
    return pl.pallas_call(
        _rms_kernel,
        grid=(pl.cdiv(m, bm),),
        in_specs=[pl.BlockSpec((bm, d), lambda i: (i, 0)),
                  pl.BlockSpec((1, d), lambda i: (0, 0))],
        out_specs=pl.BlockSpec((bm, d), lambda i: (i, 0)),
        out_shape=jax.ShapeDtypeStruct((m, d), out_dtype),
        compiler_params=_params(("parallel",), 32),
        name="rmsnorm",
    )(x, g.reshape(1, d))


def _ahead(j, i, n_j, n_i, lead):
    lead = min(lead, n_i - 1)
    if lead <= 0:
        return j
    return jnp.minimum(j + (i >= n_i - lead).astype(jnp.int32), n_j - 1)


def _round_into(dst, srcs):
    r0 = 0
    for src in srcs:
        dst[r0:r0 + src.shape[0], :] = src[...].astype(BF16)
        r0 += src.shape[0]


def _weight_specs(kparts, bn, lead0, n_j, n_i, prefix, krow0=0):
    specs = []
    for p, kp in enumerate(kparts):
        def imap(j, i, p=p):
            return prefix + (krow0 + p, _ahead(j, i, n_j, n_i, lead0 + p))
        specs.append(pl.BlockSpec((None,) * len(prefix) + (kp, bn), imap))
    return specs


def _mm_kernel(xp_ref, xs_ref, w0_ref, w1_ref, op_ref, os_ref, wb):
    @pl.when(pl.program_id(1) == 0)
    def _():
        _round_into(wb, [w0_ref, w1_ref])
        os_ref[...] = jnp.dot(xs_ref[...], wb[...], preferred_element_type=F32).astype(os_ref.dtype)

    op_ref[...] = jnp.dot(xp_ref[...], wb[...], preferred_element_type=F32).astype(op_ref.dtype)


def in_proj(xp, xs, w_all, idx, out_dtype):
    mp, k = xp.shape
    ms = xs.shape[0]
    n = w_all.shape[-1]
    bm, bn = min(mp, ROW_BLOCK), COL_BLOCK
    n_j, n_i = pl.cdiv(n, bn), mp // bm
    assert mp % bm == 0 and k % 2 == 0
    return pl.pallas_call(
        _mm_kernel,
        grid=(n_j, n_i),
        in_specs=[pl.BlockSpec((bm, k), lambda j, i: (i, 0)),
                  pl.BlockSpec((ms, k), lambda j, i: (0, 0))]
                 + _weight_specs((k // 2, k // 2), bn, 2, n_j, n_i, (idx,)),
        out_specs=[pl.BlockSpec((bm, bn), lambda j, i: (i, j)),
                   pl.BlockSpec((ms, bn), lambda j, i: (0, j))],
        out_shape=[jax.ShapeDtypeStruct((mp, n), out_dtype), jax.ShapeDtypeStruct((ms, n), out_dtype)],
        scratch_shapes=[pltpu.VMEM((k, bn), BF16)],
        compiler_params=_params(("arbitrary", "arbitrary"), 52),
        name="in_proj",
    )(xp, xs, w_all, w_all)


def _gate_up_kernel(xp_ref, xs_ref, wg_ref, wu_ref, op_ref, os_ref, wgb, wub):
    def act(x):
        g = jnp.dot(x, wgb[...], preferred_element_type=F32)
        u = jnp.dot(x, wub[...], preferred_element_type=F32)
        return (_silu(g) * u).astype(op_ref.dtype)

    @pl.when(pl.program_id(1) == 0)
    def _():
        _round_into(wgb, [wg_ref])
        _round_into(wub, [wu_ref])
        os_ref[...] = act(xs_ref[...])

    op_ref[...] = act(xp_ref[...])


def gate_up(xp, xs, wg_all, wu_all, l, j2):
    mp, k = xp.shape
    ms = xs.shape[0]
    n = wg_all.shape[-1]
    bm, bn = min(mp, ROW_BLOCK), COL_BLOCK // 2
    n_j, n_i = pl.cdiv(n, bn), mp // bm
    assert mp % bm == 0
    return pl.pallas_call(
        _gate_up_kernel,
        grid=(n_j, n_i),
        in_specs=[pl.BlockSpec((bm, k), lambda j, i: (i, 0)),
                  pl.BlockSpec((ms, k), lambda j, i: (0, 0))]
                 + _weight_specs((k,), bn, 3, n_j, n_i, (l, j2))
                 + _weight_specs((k,), bn, 2, n_j, n_i, (l, j2)),
        out_specs=[pl.BlockSpec((bm, bn), lambda j, i: (i, j)),
                   pl.BlockSpec((ms, bn), lambda j, i: (0, j))],
        out_shape=[jax.ShapeDtypeStruct((mp, n), BF16), jax.ShapeDtypeStruct((ms, n), BF16)],
        scratch_shapes=[pltpu.VMEM((k, bn), BF16), pltpu.VMEM((k, bn), BF16)],
        compiler_params=_params(("arbitrary", "arbitrary"), 52),
        name="ffn_gate_up",
    )(xp, xs, wg_all, wu_all)


def _down_kernel(hp_ref, hs_ref, w0_ref, w1_ref, xp_ref, xs_ref, op_ref, os_ref, wb, *, scale):
    @pl.when(pl.program_id(1) == 0)
    def _():
        _round_into(wb, [w0_ref, w1_ref])
        os_ref[...] = xs_ref[...] + scale * jnp.dot(hs_ref[...], wb[...], preferred_element_type=F32)

    op_ref[...] = xp_ref[...] + scale * jnp.dot(hp_ref[...], wb[...], preferred_element_type=F32)


def down_residual(hp, hs, w_all, l, j2, xp, xs, scale):
    mp, k = hp.shape
    ms = hs.shape[0]
    n = w_all.shape[-1]
    tk = k // 2
    assert tk * 2 == k and tk % 128 == 0 and (tk // 2) % 8 == 0
    bm, bn = min(mp, ROW_BLOCK // 2), COL_BLOCK
    n_j, n_i = n // bn, mp // bm
    assert mp % bm == 0 and n % bn == 0
    for kb in range(2):
        xp, xs = pl.pallas_call(
            functools.partial(_down_kernel, scale=scale),
            grid=(n_j, n_i),
            in_specs=[pl.BlockSpec((bm, tk), lambda j, i, kb=kb: (i, kb)),
                      pl.BlockSpec((ms, tk), lambda j, i, kb=kb: (0, kb))]
                     + _weight_specs((tk // 2, tk // 2), bn, 2, n_j, n_i, (l, j2), krow0=2 * kb)
                     + [pl.BlockSpec((bm, bn), lambda j, i: (i, j)),
                        pl.BlockSpec((ms, bn), lambda j, i: (0, j))],
            out_specs=[pl.BlockSpec((bm, bn), lambda j, i: (i, j)),
                       pl.BlockSpec((ms, bn), lambda j, i: (0, j))],
            out_shape=[jax.ShapeDtypeStruct((mp, n), F32), jax.ShapeDtypeStruct((ms, n), F32)],
            scratch_shapes=[pltpu.VMEM((tk, bn), BF16)],
            compiler_params=_params(("arbitrary", "arbitrary"), 54),
            name="ffn_down",
        )(hp, hs, w_all, w_all, xp, xs)
    return xp, xs


def _out_proj_kernel(ap_ref, bp_ref, as_ref, bs_ref, wa_ref, wb_ref, xp_ref, xs_ref, op_ref, os_ref, wab, wbb):
    def mix(a, b):
        return (jnp.dot(a, wab[...], preferred_element_type=F32)
                + jnp.dot(b, wbb[...], preferred_element_type=F32))

    @pl.when(pl.program_id(1) == 0)
    def _():
        _round_into(wab, [wa_ref])
        _round_into(wbb, [wb_ref])
        os_ref[...] = xs_ref[...] + mix(as_ref[...], bs_ref[...])

    op_ref[...] = xp_ref[...] + mix(ap_ref[...], bp_ref[...])


def out_proj_residual(ap, bp, as_, bs, w_all, idx, xp, xs):
    mp, ka = ap.shape
    ms = as_.shape[0]
    assert bp.shape[1] == ka and w_all.shape[1] == 2 * ka
    n = w_all.shape[-1]
    bm, bn = min(mp, ROW_BLOCK), COL_BLOCK
    n_j, n_i = n // bn, mp // bm
    assert mp % bm == 0 and n % bn == 0
    row = lambda r, c: pl.BlockSpec((r, c), lambda j, i: (i, 0))
    fix = lambda r, c: pl.BlockSpec((r, c), lambda j, i: (0, 0))
    wa_spec, wb_spec = _weight_specs((ka, ka), bn, 2, n_j, n_i, (idx,))
    return pl.pallas_call(
        _out_proj_kernel,
        grid=(n_j, n_i),
        in_specs=[row(bm, ka), row(bm, ka), fix(ms, ka), fix(ms, ka), wa_spec, wb_spec,
                  pl.BlockSpec((bm, bn), lambda j, i: (i, j)),
                  pl.BlockSpec((ms, bn), lambda j, i: (0, j))],
        out_specs=[pl.BlockSpec((bm, bn), lambda j, i: (i, j)),
                   pl.BlockSpec((ms, bn), lambda j, i: (0, j))],
        out_shape=[jax.ShapeDtypeStruct((mp, n), F32), jax.ShapeDtypeStruct((ms, n), F32)],
        scratch_shapes=[pltpu.VMEM((ka, bn), BF16), pltpu.VMEM((ka, bn), BF16)],
        compiler_params=_params(("arbitrary", "arbitrary"), 54),
        name="out_proj",
    )(ap, bp, as_, bs, w_all, w_all, xp, xs)


def _cumsum_rows(x):
    n = x.shape[0]
    row = lax.broadcasted_iota(jnp.int32, x.shape, 0)
    s = 1
    while s < n:
        x = x + jnp.where(row >= s, pltpu.roll(x, s, 0), 0.0)
        s *= 2
    return x


def _linear_scan_rows(a, u):
    n = a.shape[0]
    row = lax.broadcasted_iota(jnp.int32, a.shape, 0)
    s = 1
    while s < n:
        keep = row >= s
        a_prev = jnp.where(keep, pltpu.roll(a, s, 0), 1.0)
        u_prev = jnp.where(keep, pltpu.roll(u, s, 0), 0.0)
        u = a * u_prev + u
        a = a * a_prev
        s *= 2
    return a, u


def _retention_kernel(q_ref, k_ref, v_ref, g_ref, cos_ref, sin_ref, lg_ref, gn_ref, s0_ref,
                      o_ref, sout_ref, s_scr, dec_scr, *, chunk, n_valid, dk):
    c = pl.program_id(2)
    log_g = lg_ref[0][:, :1]

    @pl.when(c == 0)
    def _():
        s_scr[...] = s0_ref[0, 0]
        i = lax.broadcasted_iota(jnp.int32, (chunk, chunk), 0)
        j = lax.broadcasted_iota(jnp.int32, (chunk, chunk), 1)
        d = (i - j).astype(F32) * log_g
        dec_scr[...] = jnp.exp(jnp.where(i >= j, d, -jnp.inf))

    half = dk // 2
    cos = cos_ref[...]
    sin = sin_ref[...]

    def rot(x):
        x1 = x[:, :half]
        x2 = x[:, half:]
        return jnp.concatenate([x1 * cos - x2 * sin, x1 * sin + x2 * cos], axis=-1)

    q = rot(q_ref[...])
    k = rot(k_ref[...]) * (dk ** -0.5)
    v = v_ref[...]
    idx = lax.broadcasted_iota(jnp.int32, (chunk, 1), 0).astype(F32)
    inter = jnp.exp((idx + 1.0) * log_g)
    to_state = jnp.exp((n_valid - 1.0 - idx) * log_g)
    carry = jnp.exp(n_valid * log_g)

    s = s_scr[...]
    att = _bdot_nt(q, k) * dec_scr[...]
    o = _bdot(att, v) + _bdot(q, s) * inter
    s_new = carry * s + _bdot_tn(k * to_state, v)
    s_scr[...] = s_new

    mu = jnp.mean(o, axis=-1, keepdims=True)
    oc = o - mu
    var = jnp.mean(oc * oc, axis=-1, keepdims=True)
    y = oc * lax.rsqrt(var + NORM_EPS) * gn_ref[...]
    o_ref[...] = (y * _silu(g_ref[...])).astype(o_ref.dtype)

    @pl.when(c == pl.num_programs(2) - 1)
    def _():
        sout_ref[0, 0] = s_new


def retention(proj, row0, b, t, cos, sin, gn, s0, *, n_heads, dk, dv, n_valid, chunk):
    assert dk == dv and t % chunk == 0 and row0 % chunk == 0 and (n_valid == chunk or t == chunk)
    h = n_heads
    nc = t // chunk
    rb0 = row0 // chunk
    log_g = jnp.log1p(-jnp.exp2(-5.0 - jnp.arange(h, dtype=F32)))
    lg = jnp.broadcast_to(log_g[:, None, None], (h, 1, 128))
    kern = functools.partial(_retention_kernel, chunk=chunk, n_valid=n_valid, dk=dk)
    col = lambda off: pl.BlockSpec((chunk, dk), lambda bi, hi, ci, off=off: (rb0 + bi * nc + ci, off + hi))
    return pl.pallas_call(
        kern,
        grid=(b, h, nc),
        in_specs=[col(0), col(h), col(2 * h), col(3 * h),
                  pl.BlockSpec((chunk, dk // 2), lambda bi, hi, ci: (ci, 0)),
                  pl.BlockSpec((chunk, dk // 2), lambda bi, hi, ci: (ci, 0)),
                  pl.BlockSpec((1, 1, 128), lambda bi, hi, ci: (hi, 0, 0)),
                  pl.BlockSpec((1, dv), lambda bi, hi, ci: (0, hi)),
                  pl.BlockSpec((1, 1, dk, dv), lambda bi, hi, ci: (bi, hi, 0, 0))],
        out_specs=[pl.BlockSpec((chunk, dv), lambda bi, hi, ci: (bi * nc + ci, hi)),
                   pl.BlockSpec((1, 1, dk, dv), lambda bi, hi, ci: (bi, hi, 0, 0))],
        out_shape=[jax.ShapeDtypeStruct((b * t, h * dv), BF16),
                   jax.ShapeDtypeStruct((b, h, dk, dv), F32)],
        scratch_shapes=[pltpu.VMEM((dk, dv), F32), pltpu.VMEM((chunk, chunk), F32)],
        compiler_params=_params(("parallel", "parallel", "arbitrary"), 32),
        name="retention",
    )(proj, proj, proj, proj, cos, sin, lg, gn.reshape(1, h * dv), s0)


def _moba_prompt_kernel(q_ref, k_ref, v_ref, o_ref, sel_scr, kb_scr, vb_scr, *, nb, scale, hp, dh):
    i = pl.program_id(2)
    blk = MOBA_BLOCK

    @pl.when(i == 0)
    def _():
        for k in range(hp):
            sl = slice(k * dh, (k + 1) * dh)
            kk = k_ref[:, sl]
            t = kk.shape[0]
            kb_scr[k] = kk.astype(BF16)
            vb_scr[k] = v_ref[:, sl].astype(BF16)
            kmean = jnp.mean(kk.reshape(nb, blk, dh), axis=1)
            gate = lax.dot_general(kmean, q_ref[:, sl], _NT, precision=lax.Precision.HIGHEST,
                                   preferred_element_type=F32)
            bidx = lax.broadcasted_iota(jnp.int32, (nb, t), 0)
            past = bidx < lax.broadcasted_iota(jnp.int32, (nb, t), 1) // blk
            gate = jnp.where(past, gate, -jnp.inf)
            rank = jnp.zeros(gate.shape, jnp.int32)
            for mm in range(nb):
                gm = gate[mm:mm + 1, :]
                rank = rank + ((gm > gate) | ((gm == gate) & (bidx > mm))).astype(jnp.int32)
            sel_scr[k] = jnp.where((rank < MOBA_TOPK) & past, 1.0, 0.0).T

    own = pl.multiple_of(i * blk, blk)
    row = lax.broadcasted_iota(jnp.int32, (blk, blk), 0)
    colk = lax.broadcasted_iota(jnp.int32, (blk, blk), 1)
    causal = colk <= row

    def attend(nkb):
        outs = []
        for k in range(hp):
            qb = q_ref[pl.ds(own, blk), k * dh:(k + 1) * dh].astype(BF16)
            sel = sel_scr[k, pl.ds(own, blk), :]
            s = _bdot_nt(qb, kb_scr[k, :nkb * blk, :]) * scale
            pieces = []
            for n in range(nkb):
                is_own = jnp.full((blk, blk), n, jnp.int32) == i
                allowed = (jnp.broadcast_to(sel[:, n:n + 1], (blk, blk)) > 0.5) | (is_own & causal)
                pieces.append(jnp.where(allowed, s[:, n * blk:(n + 1) * blk], -jnp.inf))
            s = jnp.concatenate(pieces, axis=-1)
            m = jnp.max(s, axis=-1, keepdims=True)
            p = jnp.exp(s - m)
            l = jnp.sum(p, axis=-1, keepdims=True)
            outs.append(_bdot(p, vb_scr[k, :nkb * blk, :]) / l)
        o_ref[...] = jnp.concatenate(outs, axis=-1).astype(o_ref.dtype)

    step = 2 if nb % 2 == 0 else 1
    for nkb in range(step, nb + 1, step):
        @pl.when((i >= nkb - step) & (i < nkb))
        def _(nkb=nkb):
            attend(nkb)


def moba_prompt(proj, b, t, *, n_heads, dh, q_off, k_off, v_off):
    hp = MOBA_HEADS_PER_STEP
    assert t % MOBA_BLOCK == 0 and n_heads % hp == 0 and q_off % hp == 0 and k_off % hp == 0 and v_off % hp == 0
    nb = t // MOBA_BLOCK
    kern = functools.partial(_moba_prompt_kernel, nb=nb, scale=dh ** -0.5, hp=hp, dh=dh)
    w = hp * dh
    return pl.pallas_call(
        kern,
        grid=(b, n_heads // hp, nb),
        in_specs=[pl.BlockSpec((t, w), lambda bi, hi, ii: (bi, q_off // hp + hi)),
                  pl.BlockSpec((t, w), lambda bi, hi, ii: (bi, k_off // hp + hi)),
                  pl.BlockSpec((t, w), lambda bi, hi, ii: (bi, v_off // hp + hi))],
        out_specs=pl.BlockSpec((MOBA_BLOCK, w), lambda bi, hi, ii: (bi * nb + ii, hi)),
        out_shape=jax.ShapeDtypeStruct((b * t, n_heads * dh), BF16),
        scratch_shapes=[pltpu.VMEM((hp, t, nb), F32), pltpu.VMEM((hp, t, dh), BF16), pltpu.VMEM((hp, t, dh), BF16)],
        compiler_params=_params(("parallel", "parallel", "arbitrary"), 48),
        name="moba_prompt",
    )(proj, proj, proj)


def _moba_decode_kernel(pt_ref, q_ref, knew_ref, vnew_ref, *refs, n_heads, nblk, ts, ppb, scale):
    k_refs, v_refs = refs[:ppb], refs[ppb:2 * ppb]
    o_ref, km_scr, bias_scr, mb_scr, lb_scr, accb_scr = refs[2 * ppb:]
    n = pl.program_id(1)
    q = q_ref[0]
    rows, dh = q.shape
    qb = q.astype(BF16)

    @pl.when(n == 0)
    def _():
        r = lax.broadcasted_iota(jnp.int32, bias_scr.shape, 0)
        c = lax.broadcasted_iota(jnp.int32, bias_scr.shape, 1)
        bias_scr[...] = jnp.where((r % n_heads) == (c % n_heads), 0.0, -jnp.inf)

    ksum = jnp.sum(k_refs[0][...], axis=0)
    for k_ref in k_refs[1:]:
        ksum = ksum + jnp.sum(k_ref[...], axis=0)
    km_scr[n] = ksum * (1.0 / MOBA_BLOCK)

    bias = bias_scr[...]
    scores = [_bdot_nt(qb, k_ref[...].reshape(-1, dh)) * scale + bias for k_ref in k_refs]
    m_n = jnp.max(scores[0], axis=-1, keepdims=True)
    for s in scores[1:]:
        m_n = jnp.maximum(m_n, jnp.max(s, axis=-1, keepdims=True))
    l_n = jnp.zeros((rows, 1), F32)
    acc_n = jnp.zeros((rows, dh), F32)
    for s, v_ref in zip(scores, v_refs):
        e = jnp.exp(s - m_n)
        l_n = l_n + jnp.sum(e, axis=-1, keepdims=True)
        acc_n = acc_n + _bdot(e, v_ref[...].reshape(-1, dh))
    mb_scr[n] = m_n
    lb_scr[n] = l_n
    accb_scr[n] = acc_n

    @pl.when(n == nblk - 1)
    def _():
        km = km_scr[...].reshape(nblk * n_heads, dh)
        g = lax.dot_general(q, km, _NT, precision=lax.Precision.HIGHEST, preferred_element_type=F32)
        width = nblk * n_heads
        rs = lax.broadcasted_iota(jnp.int32, (rows, width), 0)
        cs = lax.broadcasted_iota(jnp.int32, (rows, width), 1)
        nidx = cs // n_heads
        rank = jnp.zeros((rows, width), jnp.int32)
        for k in range(1, nblk):
            other = pltpu.roll(g, k * n_heads, 1)
            rank = rank + ((other > g) | ((other == g) & (nidx >= k))).astype(jnp.int32)
        sel = jnp.where((rank < MOBA_TOPK) & ((cs % n_heads) == (rs % n_heads)), 1.0, 0.0)
        flags = [jnp.sum(sel[:, blk * n_heads:(blk + 1) * n_heads], axis=-1, keepdims=True) > 0.5
                 for blk in range(nblk)]

        kn = knew_ref[0].reshape(-1, dh)
        vn = vnew_ref[0].reshape(-1, dh)
        r = lax.broadcasted_iota(jnp.int32, (rows, kn.shape[0]), 0)
        c = lax.broadcasted_iota(jnp.int32, (rows, kn.shape[0]), 1)
        tj = c // n_heads
        ok = ((r % n_heads) == (c % n_heads)) & (tj <= r // n_heads) & (tj < ts)
        s = jnp.where(ok, _bdot_nt(qb, kn) * scale, -jnp.inf)
        m_own = jnp.max(s, axis=-1, keepdims=True)

        m = m_own
        for blk in range(nblk):
            m = jnp.maximum(m, jnp.where(flags[blk], mb_scr[blk], -jnp.inf))
        e = jnp.exp(s - m)
        l = jnp.sum(e, axis=-1, keepdims=True)
        acc = _bdot(e, vn)
        for blk in range(nblk):
            w = jnp.where(flags[blk], jnp.exp(mb_scr[blk] - m), 0.0)
            l = l + w * lb_scr[blk]
            acc = acc + w * accb_scr[blk]
        o_ref[0] = acc / l


def moba_sample(qb, knew, vnew, cache_k, cache_v, layer, page_table, *, n_heads, dh):
    b, ts, w = qb.shape
    n_pages = page_table.shape[1]
    page = cache_k.shape[2]
    past = n_pages * page
    assert past % MOBA_BLOCK == 0 and MOBA_BLOCK % page == 0 and ts <= MOBA_BLOCK and w == n_heads * dh
    ppb = MOBA_BLOCK // page
    nblk = past // MOBA_BLOCK
    rows = ts * n_heads
    ts_pad = -(-ts // 8) * 8
    pad = ((0, 0), (0, ts_pad - ts), (0, 0), (0, 0))
    q3 = qb.reshape(b, rows, dh)
    kn = jnp.pad(knew.reshape(b, ts, n_heads, dh), pad)
    vn = jnp.pad(vnew.reshape(b, ts, n_heads, dh), pad)
    page_specs = [pl.BlockSpec((None, None, page, n_heads, dh),
                               lambda bi, ni, pt, pg=pg: (layer, pt[bi, ni * ppb + pg], 0, 0, 0))
                  for pg in range(ppb)]

    out = pl.pallas_call(
        functools.partial(_moba_decode_kernel, n_heads=n_heads, nblk=nblk, ts=ts, ppb=ppb, scale=dh ** -0.5),
        grid_spec=pltpu.PrefetchScalarGridSpec(
            num_scalar_prefetch=1,
            grid=(b, nblk),
            in_specs=[pl.BlockSpec((1, rows, dh), lambda bi, ni, pt: (bi, 0, 0)),
                      pl.BlockSpec((1, ts_pad, n_heads, dh), lambda bi, ni, pt: (bi, 0, 0, 0)),
                      pl.BlockSpec((1, ts_pad, n_heads, dh), lambda bi, ni, pt: (bi, 0, 0, 0))]
                     + page_specs + page_specs,
            out_specs=pl.BlockSpec((1, rows, dh), lambda bi, ni, pt: (bi, 0, 0)),
            scratch_shapes=[pltpu.VMEM((nblk, n_heads, dh), F32), pltpu.VMEM((rows, page * n_heads), F32),
                            pltpu.VMEM((nblk, rows, 1), F32), pltpu.VMEM((nblk, rows, 1), F32),
                            pltpu.VMEM((nblk, rows, dh), F32)]),
        out_shape=jax.ShapeDtypeStruct((b, rows, dh), F32),
        compiler_params=_params(("parallel", "arbitrary"), 40),
        name="moba_sample_attn",
    )(page_table, q3, kn, vn, *([cache_k] * ppb), *([cache_v] * ppb))
    return out.reshape(b, ts, w)


def _rglru_kernel(x_ref, g_ref, tail0_ref, h0_ref, cw_ref, cb_ref, wr_ref, br_ref, wi_ref, bi_ref, lam_ref,
                  o_ref, hlast_ref, tail_scr, h_scr, *, tc, n_valid):
    c = pl.program_id(2)

    @pl.when(c == 0)
    def _():
        tail_scr[...] = tail0_ref[0]
        h_scr[...] = h0_ref[0]

    x = x_ref[...]
    cw = cw_ref[...]
    taps = cw.shape[0]
    ext = jnp.concatenate([tail_scr[...], x], axis=0)
    xc = cb_ref[...] + x * cw[taps - 1:taps, :]
    for s in range(1, taps):
        xc = xc + pltpu.roll(ext, s, 0)[8:, :] * cw[taps - 1 - s:taps - s, :]
    tail_scr[...] = x[tc - 8:, :]

    nblk = x.shape[1] // GATE_BLOCK
    zr, zi = [], []
    for kb in range(nblk):
        xb = xc[:, kb * GATE_BLOCK:(kb + 1) * GATE_BLOCK]
        zr.append(_bdot(xb, wr_ref[kb]))
        zi.append(_bdot(xb, wi_ref[kb]))
    r = jax.nn.sigmoid(jnp.concatenate(zr, axis=-1) + br_ref[...])
    ig = jax.nn.sigmoid(jnp.concatenate(zi, axis=-1) + bi_ref[...])
    lam = lam_ref[...]
    softplus_neg = jnp.maximum(-lam, 0.0) + jnp.log1p(jnp.exp(-jnp.abs(lam)))
    log_a = -RG_C * r * softplus_neg
    a = jnp.exp(log_a)
    u = jnp.sqrt(jnp.tanh(-log_a) * (1.0 + a * a)) * ig * xc
    a_cum, hz = _linear_scan_rows(a, u)
    h = hz + a_cum * h_scr[...]
    h_scr[...] = h[n_valid - 1:n_valid, :]
    o_ref[...] = (h * jax.nn.gelu(g_ref[...])).astype(o_ref.dtype)

    @pl.when(c == pl.num_programs(2) - 1)
    def _():
        hlast_ref[0] = h[n_valid - 1:n_valid, :]


def rglru(proj, row0, b, t, conv_buf, h0, conv_w, conv_b, w_r, b_r, w_i, b_i, lam, *, width, n_valid, tc, bw=512):
    taps = conv_w.shape[0]
    assert t % tc == 0 and tc % 8 == 0 and row0 % tc == 0 and (n_valid == tc or t == tc) and taps - 1 <= 8
    nw = width // bw
    gpb = bw // GATE_BLOCK
    nc = t // tc
    rb0 = row0 // tc
    tail0 = jnp.pad(conv_buf, ((0, 0), (8 - (taps - 1), 0), (0, 0)))
    vec = lambda a: a.reshape(1, width)
    vspec = pl.BlockSpec((1, bw), lambda bi, wi, ci: (0, wi))
    kern = functools.partial(_rglru_kernel, tc=tc, n_valid=n_valid)
    return pl.pallas_call(
        kern,
        grid=(b, nw, nc),
        in_specs=[pl.BlockSpec((tc, bw), lambda bi, wi, ci: (rb0 + bi * nc + ci, wi)),
                  pl.BlockSpec((tc, bw), lambda bi, wi, ci: (rb0 + bi * nc + ci, nw + wi)),
                  pl.BlockSpec((1, 8, bw), lambda bi, wi, ci: (bi, 0, wi)),
                  pl.BlockSpec((1, 1, bw), lambda bi, wi, ci: (bi, 0, wi)),
                  pl.BlockSpec((taps, bw), lambda bi, wi, ci: (0, wi)),
                  vspec,
                  pl.BlockSpec((gpb, GATE_BLOCK, GATE_BLOCK), lambda bi, wi, ci: (wi, 0, 0)),
                  vspec,
                  pl.BlockSpec((gpb, GATE_BLOCK, GATE_BLOCK), lambda bi, wi, ci: (wi, 0, 0)),
                  vspec, vspec],
        out_specs=[pl.BlockSpec((tc, bw), lambda bi, wi, ci: (bi * nc + ci, wi)),
                   pl.BlockSpec((1, 1, bw), lambda bi, wi, ci: (bi, 0, wi))],
        out_shape=[jax.ShapeDtypeStruct((b * t, width), BF16),
                   jax.ShapeDtypeStruct((b, 1, width), F32)],
        scratch_shapes=[pltpu.VMEM((8, bw), F32), pltpu.VMEM((1, bw), F32)],
        compiler_params=_params(("parallel", "parallel", "arbitrary"), 32),
        name="rglru",
    )(proj, proj, tail0, h0.reshape(b, 1, width), conv_w, vec(conv_b), w_r, vec(b_r), w_i, vec(b_i), vec(lam))


def _hgrn_head(q, fz, v, g, lb, ng, st, *, chunk, n_valid):
    dk = q.shape[1]
    log_f = jnp.log(lb + (1.0 - lb) * jax.nn.sigmoid(fz))
    key = (1.0 - lb) * jax.nn.sigmoid(-fz)
    if n_valid < chunk:
        live = lax.broadcasted_iota(jnp.int32, (chunk, dk), 0) < n_valid
        log_f = jnp.where(live, log_f, 0.0)
        key = jnp.where(live, key, 0.0)
    cum = _cumsum_rows(log_f)
    last = cum[chunk - 1:chunk, :]

    o = _bdot_nt(q * jnp.exp(cum), st)

    row = lax.broadcasted_iota(jnp.int32, (chunk, dk), 0)
    ri = lax.broadcasted_iota(jnp.int32, (chunk, chunk), 0)
    ci = lax.broadcasted_iota(jnp.int32, (chunk, chunk), 1)
    att = None
    s = chunk // 2
    while s >= SUB:
        grp = chunk // (2 * s)
        edge = jnp.broadcast_to(cum.reshape(grp, 2 * s, dk)[:, s - 1:s, :], (grp, 2 * s, dk)).reshape(chunk, dk)
        upper = (row % (2 * s)) >= s
        qs = q * jnp.exp(jnp.where(upper, cum - edge, -jnp.inf))
        ks = key * jnp.exp(jnp.where(upper, -jnp.inf, edge - cum))
        a = _bdot_nt(qs, ks)
        if grp > 1:
            a = jnp.where((ri // (2 * s)) == (ci // (2 * s)), a, 0.0)
        att = a if att is None else att + a
        s //= 2
    if att is not None:
        o = o + _bdot(att, v)

    nsb = chunk // SUB
    q3 = q.reshape(nsb, SUB, dk)
    k3 = key.reshape(nsb, SUB, dk)
    c3 = cum.reshape(nsb, SUB, dk)
    v3 = v.reshape(nsb, SUB, v.shape[1])
    ri = lax.broadcasted_iota(jnp.int32, (nsb, SUB, dk), 1)
    od = jnp.zeros(v3.shape, F32)
    for j in range(SUB):
        d = jnp.where(ri >= j, c3 - c3[:, j:j + 1, :], -jnp.inf)
        wgt = jnp.sum(q3 * jnp.exp(d) * k3[:, j:j + 1, :], axis=-1, keepdims=True)
        od = od + wgt * v3[:, j:j + 1, :]
    o = o + od.reshape(chunk, v.shape[1])

    st_new = st * jnp.exp(last) + _bdot_tn(v, key * jnp.exp(last - cum))
    y = o * lax.rsqrt(jnp.mean(o * o, axis=-1, keepdims=True) + NORM_EPS) * ng
    return y * _silu(g), st_new


def _hgrn_kernel(q_ref, f_ref, i_ref, g_ref, lbl_ref, ng_ref, s0_ref, o_ref, sout_ref, st_scr,
                 *, chunk, n_valid, layer, hp, dk):
    c = pl.program_id(2)

    @pl.when(c == 0)
    def _():
        for k in range(hp):
            st_scr[k] = s0_ref[0, k].T

    lbl = lbl_ref[...]
    e = jnp.exp(lbl - jnp.max(lbl, axis=0, keepdims=True))
    soft = e / jnp.sum(e, axis=0, keepdims=True)
    lb_all = jnp.zeros((1, lbl.shape[1]), F32)
    for r in range(1, layer + 1):
        lb_all = lb_all + soft[r:r + 1, :]

    outs, states = [], []
    for k in range(hp):
        sl = slice(k * dk, (k + 1) * dk)
        y, st_new = _hgrn_head(q_ref[:, sl], f_ref[:, sl], i_ref[:, sl], g_ref[:, sl], lb_all[:, sl],
                               ng_ref[:, sl], st_scr[k], chunk=chunk, n_valid=n_valid)
        st_scr[k] = st_new
        outs.append(y)
        states.append(st_new)
    o_ref[...] = jnp.concatenate(outs, axis=-1).astype(o_ref.dtype)

    @pl.when(c == pl.num_programs(2) - 1)
    def _():
        for k in range(hp):
            sout_ref[0, k] = states[k].T


def hgrn2(proj, row0, b, t, lb_logits, norm_g, s0, *, layer, n_heads, dk, dv, col0, n_valid, chunk=128):
    hp = HGRN_HEADS_PER_STEP
    assert dk == dv and t % chunk == 0 and chunk % SUB == 0 and row0 % chunk == 0 and (n_valid == chunk or t == chunk)
    assert n_heads % hp == 0 and col0 % hp == 0
    h = n_heads
    nc = t // chunk
    rb0 = row0 // chunk
    kern = functools.partial(_hgrn_kernel, chunk=chunk, n_valid=n_valid, layer=layer, hp=hp, dk=dk)
    col = lambda off: pl.BlockSpec((chunk, hp * dk),
                                   lambda bi, hi, ci, off=off: (rb0 + bi * nc + ci, (col0 + off) // hp + hi))
    nl = lb_logits.shape[0]
    return pl.pallas_call(
        kern,
        grid=(b, h // hp, nc),
        in_specs=[col(0), col(h), col(2 * h), col(3 * h),
                  pl.BlockSpec((nl, hp * dk), lambda bi, hi, ci: (0, hi)),
                  pl.BlockSpec((1, hp * dv), lambda bi, hi, ci: (0, hi)),
                  pl.BlockSpec((1, hp, dk, dv), lambda bi, hi, ci: (bi, hi, 0, 0))],
        out_specs=[pl.BlockSpec((chunk, hp * dv), lambda bi, hi, ci: (bi * nc + ci, hi)),
                   pl.BlockSpec((1, hp, dk, dv), lambda bi, hi, ci: (bi, hi, 0, 0))],
        out_shape=[jax.ShapeDtypeStruct((b * t, h * dv), BF16),
                   jax.ShapeDtypeStruct((b, h, dk, dv), F32)],
        scratch_shapes=[pltpu.VMEM((hp, dv, dk), F32)],
        compiler_params=_params(("parallel", "parallel", "arbitrary"), 32),
        name="hgrn2",
    )(proj, proj, proj, proj, lb_logits, norm_g.reshape(1, h * dv), s0)


def _rope_tables(pos, half):
    freq = ROPE_BASE ** (-jnp.arange(half, dtype=F32) / half)
    ang = pos.astype(F32)[:, None] * freq[None, :]
    return jnp.cos(ang), jnp.sin(ang)


def _pad_seq(x, b, t, t_pad):
    c = x.shape[1]
    return jnp.pad(x.reshape(b, t, c), ((0, 0), (0, t_pad - t), (0, 0))).reshape(b * t_pad, c)


def _unpad_seq(x, b, t, t_pad):
    return x.reshape(b, t_pad, x.shape[1])[:, :t].reshape(b * t, x.shape[1])


def kernel(x_prompt, x_sample, state_ret, cache_k, cache_v, state_rglru, state_conv, state_hgrn, page_table, norm_ffn1, norm_mix, norm_ffn2, ffn_gate, ffn_up, ffn_down, w_in_ab, w_out_ab, gn_ret, w_in_cd, w_out_cd, conv_w, conv_b, w_rgate, b_rgate, w_igate, b_igate, lru_lambda, hgrn_lb_logits, hgrn_norm, final_norm):
    depth = norm_ffn1.shape[0]
    d_model = x_prompt.shape[-1]
    bp, tp = x_prompt.shape[0], x_prompt.shape[1]
    bs, ts = x_sample.shape[0], x_sample.shape[1]
    mp, ms = bp * tp, bs * ts
    n_ab, _, h_a, dk_a, dv_a = state_ret.shape
    h_b, dh_b = cache_k.shape[3], cache_k.shape[4]
    n_cd, _, w_c = state_rglru.shape
    _, _, h_d, dk_d, dv_d = state_hgrn.shape
    taps = conv_w.shape[1]
    past_len = page_table.shape[1] * cache_k.shape[2]
    a_cols = h_a * dk_a
    b_cols = h_b * dh_b
    b_off = 4 * a_cols // dh_b

    wg, wu, wd = ffn_gate, ffn_up, ffn_down
    win_ab, wout_ab, win_cd, wout_cd = w_in_ab, w_out_ab, w_in_cd, w_out_cd

    ts_ret, ts_lru, ts_hg = 128, 8, 128
    ret_chunk_p = min(tp, 256)
    cos_p, sin_p = _rope_tables(jnp.arange(tp), dk_a // 2)
    cos_s, sin_s = _rope_tables(past_len + jnp.arange(ts_ret), dk_a // 2)
    zeros = lambda *s: jnp.zeros(s, F32)

    def ffn(xp, xs, l, j2, g):
        hp, hs = gate_up(rmsnorm(xp, g, BF16), rmsnorm(xs, g, BF16), wg, wu, l, j2)
        return down_residual(hp, hs, wd, l, j2, xp, xs, 0.5)

    xp = x_prompt.reshape(mp, d_model)
    xs = x_sample.reshape(ms, d_model)
    ret_p, ret_s, k_p, k_s, v_p, v_s = [], [], [], [], [], []
    h_p, h_s, buf_p, buf_s, hg_p, hg_s = [], [], [], [], [], []
    for l in range(depth):
        i = l // 2
        xp, xs = ffn(xp, xs, l, 0, norm_ffn1[l])
        hn_p = rmsnorm(xp, norm_mix[l], BF16)
        hn_s = rmsnorm(xs, norm_mix[l], BF16)
        if l % 2 == 0:
            proj, proj_s = in_proj(hn_p, hn_s, win_ab, i, F32)
            kb_cols = slice(4 * a_cols + b_cols, 4 * a_cols + 2 * b_cols)
            vb_cols = slice(4 * a_cols + 2 * b_cols, 4 * a_cols + 3 * b_cols)
            k_p.append(proj[:, kb_cols].reshape(bp, tp, h_b, dh_b))
            v_p.append(proj[:, vb_cols].reshape(bp, tp, h_b, dh_b))
            k_s.append(proj_s[:, kb_cols].reshape(bs, ts, h_b, dh_b))
            v_s.append(proj_s[:, vb_cols].reshape(bs, ts, h_b, dh_b))

            oa_p, s_p = retention(proj, 0, bp, tp, cos_p, sin_p, gn_ret[i], zeros(bp, h_a, dk_a, dv_a),
                                  n_heads=h_a, dk=dk_a, dv=dv_a, n_valid=ret_chunk_p, chunk=ret_chunk_p)
            oa_s, s_s = retention(_pad_seq(proj_s[:, :4 * a_cols], bs, ts, ts_ret), 0, bs, ts_ret, cos_s, sin_s,
                                  gn_ret[i], state_ret[i], n_heads=h_a, dk=dk_a, dv=dv_a, n_valid=ts, chunk=ts_ret)
            ob_p = moba_prompt(proj, bp, tp, n_heads=h_b, dh=dh_b, q_off=b_off, k_off=b_off + h_b, v_off=b_off + 2 * h_b)
            ob_s = moba_sample(proj_s[:, 4 * a_cols:4 * a_cols + b_cols].reshape(bs, ts, b_cols),
                               proj_s[:, kb_cols].reshape(bs, ts, b_cols), proj_s[:, vb_cols].reshape(bs, ts, b_cols),
                               cache_k, cache_v, i, page_table, n_heads=h_b, dh=dh_b)
            xp, xs = out_proj_residual(oa_p, ob_p, _unpad_seq(oa_s, bs, ts, ts_ret),
                                       ob_s.reshape(ms, b_cols).astype(BF16), wout_ab, i, xp, xs)
            ret_p.append(s_p)
            ret_s.append(s_s)
        else:
            proj, proj_s = in_proj(hn_p, hn_s, win_cd, i, F32)
            xb_p = proj[:, :w_c].reshape(bp, tp, w_c)
            xb_s = proj_s[:, :w_c].reshape(bs, ts, w_c)
            buf_p.append(jnp.concatenate([zeros(bp, taps - 1, w_c), xb_p], axis=1)[:, tp:])
            buf_s.append(jnp.concatenate([state_conv[i], xb_s], axis=1)[:, ts:])

            lru = functools.partial(rglru, conv_w=conv_w[i], conv_b=conv_b[i], w_r=w_rgate[i], b_r=b_rgate[i],
                                    w_i=w_igate[i], b_i=b_igate[i], lam=lru_lambda[i], width=w_c)
            oc_p, hl_p = lru(proj, 0, bp, tp, zeros(bp, taps - 1, w_c), zeros(bp, w_c), n_valid=min(tp, 256), tc=min(tp, 256))
            oc_s, hl_s = lru(_pad_seq(proj_s[:, :2 * w_c], bs, ts, ts_lru), 0, bs, ts_lru, state_conv[i], state_rglru[i],
                             n_valid=ts, tc=ts_lru)
            hg = functools.partial(hgrn2, lb_logits=hgrn_lb_logits, norm_g=hgrn_norm[i], layer=i, n_heads=h_d,
                                   dk=dk_d, dv=dv_d, col0=2 * w_c // dk_d, chunk=128)
            od_p, sg_p = hg(proj, 0, bp, tp, s0=zeros(bp, h_d, dk_d, dv_d), n_valid=128)
            od_s, sg_s = hg(_pad_seq(proj_s, bs, ts, ts_hg), 0, bs, ts_hg, s0=state_hgrn[i], n_valid=ts)
            xp, xs = out_proj_residual(oc_p, od_p, _unpad_seq(oc_s, bs, ts, ts_lru), _unpad_seq(od_s, bs, ts, ts_hg),
                                       wout_cd, i, xp, xs)
            h_p.append(hl_p.reshape(bp, w_c))
            h_s.append(hl_s.reshape(bs, w_c))
            hg_p.append(sg_p)
            hg_s.append(sg_s)
        xp, xs = ffn(xp, xs, l, 1, norm_ffn2[l])
    y_p = rmsnorm(xp, final_norm, F32)
    y_s = rmsnorm(xs, final_norm, F32)
    st = jnp.stack
    return (y_p.reshape(bp, tp, d_model), y_s.reshape(bs, ts, d_model),
            st(ret_p), st(k_p), st(v_p), st(h_p), st(buf_p), st(hg_p),
            st(ret_s), st(k_s), st(v_s), st(h_s), st(buf_s), st(hg_s))
```

```python
import functools

import jax
import jax.numpy as jnp
from jax import lax
from jax.experimental import pallas as pl
from jax.experimental.pallas import tpu as pltpu

F32 = jnp.float32
BF16 = jnp.bfloat16

NORM_EPS = 1e-6
ROPE_BASE = 10000.0
MOBA_BLOCK = 256
MOBA_TOPK = 3
RG_C = 8.0
GATE_BLOCK = 128
SUB = 8
HGRN_HEADS_PER_STEP = 8
MOBA_HEADS_PER_STEP = 4
MIB = 1024 * 1024
ROW_BLOCK = 1024
COL_BLOCK = 512

_NT = (((1,), (1,)), ((), ()))
_TN = (((0,), (0,)), ((), ()))


def _params(semantics, vmem_mib):
    return pltpu.CompilerParams(dimension_semantics=semantics,
                                vmem_limit_bytes=int(vmem_mib * MIB))


def _bdot(a, b):
    return jnp.dot(a.astype(BF16), b.astype(BF16), preferred_element_type=F32)


def _bdot_nt(a, b):
    return lax.dot_general(a.astype(BF16), b.astype(BF16), _NT, preferred_element_type=F32)


def _bdot_tn(a, b):
    return lax.dot_general(a.astype(BF16), b.astype(BF16), _TN, preferred_element_type=F32)


def _silu(x):
    return x * jax.nn.sigmoid(x)


def _rms_kernel(x_ref, g_ref, o_ref):
    x = x_ref[...]
    ms = jnp.mean(x * x, axis=-1, keepdims=True)
    o_ref[...] = (x * lax.rsqrt(ms + NORM_EPS) * g_ref[...]).astype(o_ref.dtype)


def rmsnorm(x, g, out_dtype):
    m, d = x.shape
    bm = min(m, 256)
    return pl.pallas_call(
        _rms_kernel,
        grid=(pl.cdiv(m, bm),),
        in_specs=[pl.BlockSpec((bm, d), lambda i: (i, 0)),
                  pl.BlockSpec((1, d), lambda i: (0, 0))],
        out_specs=pl.BlockSpec((bm, d), lambda i: (i, 0)),
        out_shape=jax.ShapeDtypeStruct((m, d), out_dtype),
        compiler_params=_params(("parallel",), 32),
        name="rmsnorm",
    )(x, g.reshape(1, d))


def _ahead(j, i, n_j, n_i, lead):
    lead = min(lead, n_i - 1)
    if lead <= 0:
        return j
    return jnp.minimum(j + (i >= n_i - lead).astype(jnp.int32), n_j - 1)


def _round_into(dst, srcs):
    r0 = 0
    for src in srcs:
        dst[r0:r0 + src.shape[0], :] = src[...].astype(BF16)
        r0 += src.shape[0]


def _weight_specs(kparts, bn, lead0, n_j, n_i, prefix, krow0=0):
    specs = []
    for p, kp in enumerate(kparts):
        def imap(j, i, p=p):
            return prefix + (krow0 + p, _ahead(j, i, n_j, n_i, lead0 + p))
        specs.append(pl.BlockSpec((None,) * len(prefix) + (kp, bn), imap))
    return specs


def _mm_kernel(xp_ref, xs_ref, w0_ref, w1_ref, op_ref, os_ref, wb):
    @pl.when(pl.program_id(1) == 0)
    def _():
        _round_into(wb, [w0_ref, w1_ref])
        os_ref[...] = jnp.dot(xs_ref[...], wb[...], preferred_element_type=F32).astype(os_ref.dtype)

    op_ref[...] = jnp.dot(xp_ref[...], wb[...], preferred_element_type=F32).astype(op_ref.dtype)


def in_proj(xp, xs, w_all, idx, out_dtype):
    mp, k = xp.shape
    ms = xs.shape[0]
    n = w_all.shape[-1]
    bm, bn = min(mp, ROW_BLOCK), COL_BLOCK
    n_j, n_i = pl.cdiv(n, bn), mp // bm
    assert mp % bm == 0 and k % 2 == 0
    return pl.pallas_call(
        _mm_kernel,
        grid=(n_j, n_i),
        in_specs=[pl.BlockSpec((bm, k), lambda j, i: (i, 0)),
                  pl.BlockSpec((ms, k), lambda j, i: (0, 0))]
                 + _weight_specs((k // 2, k // 2), bn, 2, n_j, n_i, (idx,)),
        out_specs=[pl.BlockSpec((bm, bn), lambda j, i: (i, j)),
                   pl.BlockSpec((ms, bn), lambda j, i: (0, j))],
        out_shape=[jax.ShapeDtypeStruct((mp, n), out_dtype), jax.ShapeDtypeStruct((ms, n), out_dtype)],
        scratch_shapes=[pltpu.VMEM((k, bn), BF16)],
        compiler_params=_params(("arbitrary", "arbitrary"), 52),
        name="in_proj",
    )(xp, xs, w_all, w_all)


def _gate_up_kernel(xp_ref, xs_ref, wg_ref, wu_ref, op_ref, os_ref, wgb, wub):
    def act(x):
        g = jnp.dot(x, wgb[...], preferred_element_type=F32)
        u = jnp.dot(x, wub[...], preferred_element_type=F32)
        return (_silu(g) * u).astype(op_ref.dtype)

    @pl.when(pl.program_id(1) == 0)
    def _():
        _round_into(wgb, [wg_ref])
        _round_into(wub, [wu_ref])
        os_ref[...] = act(xs_ref[...])

    op_ref[...] = act(xp_ref[...])


def gate_up(xp, xs, wg_all, wu_all, l, j2):
    mp, k = xp.shape
    ms = xs.shape[0]
    n = wg_all.shape[-1]
    bm, bn = min(mp, ROW_BLOCK), COL_BLOCK // 2
    n_j, n_i = pl.cdiv(n, bn), mp // bm
    assert mp % bm == 0
    return pl.pallas_call(
        _gate_up_kernel,
        grid=(n_j, n_i),
        in_specs=[pl.BlockSpec((bm, k), lambda j, i: (i, 0)),
                  pl.BlockSpec((ms, k), lambda j, i: (0, 0))]
                 + _weight_specs((k,), bn, 3, n_j, n_i, (l, j2))
                 + _weight_specs((k,), bn, 2, n_j, n_i, (l, j2)),
        out_specs=[pl.BlockSpec((bm, bn), lambda j, i: (i, j)),
                   pl.BlockSpec((ms, bn), lambda j, i: (0, j))],
        out_shape=[jax.ShapeDtypeStruct((mp, n), BF16), jax.ShapeDtypeStruct((ms, n), BF16)],
        scratch_shapes=[pltpu.VMEM((k, bn), BF16), pltpu.VMEM((k, bn), BF16)],
        compiler_params=_params(("arbitrary", "arbitrary"), 52),
        name="ffn_gate_up",
    )(xp, xs, wg_all, wu_all)


def _down_kernel(hp_ref, hs_ref, w0_ref, w1_ref, xp_ref, xs_ref, op_ref, os_ref, wb, *, scale):
    @pl.when(pl.program_id(1) == 0)
    def _():
        _round_into(wb, [w0_ref, w1_ref])
        os_ref[...] = xs_ref[...] + scale * jnp.dot(hs_ref[...], wb[...], preferred_element_type=F32)

    op_ref[...] = xp_ref[...] + scale * jnp.dot(hp_ref[...], wb[...], preferred_element_type=F32)


def down_residual(hp, hs, w_all, l, j2, xp, xs, scale):
    mp, k = hp.shape
    ms = hs.shape[0]
    n = w_all.shape[-1]
    tk = k // 2
    assert tk * 2 == k and tk % 128 == 0 and (tk // 2) % 8 == 0
    bm, bn = min(mp, ROW_BLOCK // 2), COL_BLOCK
    n_j, n_i = n // bn, mp // bm
    assert mp % bm == 0 and n % bn == 0
    for kb in range(2):
        xp, xs = pl.pallas_call(
            functools.partial(_down_kernel, scale=scale),
            grid=(n_j, n_i),
            in_specs=[pl.BlockSpec((bm, tk), lambda j, i, kb=kb: (i, kb)),
                      pl.BlockSpec((ms, tk), lambda j, i, kb=kb: (0, kb))]
                     + _weight_specs((tk // 2, tk // 2), bn, 2, n_j, n_i, (l, j2), krow0=2 * kb)
                     + [pl.BlockSpec((bm, bn), lambda j, i: (i, j)),
                        pl.BlockSpec((ms, bn), lambda j, i: (0, j))],
            out_specs=[pl.BlockSpec((bm, bn), lambda j, i: (i, j)),
                       pl.BlockSpec((ms, bn), lambda j, i: (0, j))],
            out_shape=[jax.ShapeDtypeStruct((mp, n), F32), jax.ShapeDtypeStruct((ms, n), F32)],
            scratch_shapes=[pltpu.VMEM((tk, bn), BF16)],
            compiler_params=_params(("arbitrary", "arbitrary"), 54),
            name="ffn_down",
        )(hp, hs, w_all, w_all, xp, xs)
    return xp, xs


def _out_proj_kernel(ap_ref, bp_ref, as_ref, bs_ref, wa_ref, wb_ref, xp_ref, xs_ref, op_ref, os_ref, wab, wbb):
    def mix(a, b):
        return (jnp.dot(a, wab[...], preferred_element_type=F32)
                + jnp.dot(b, wbb[...], preferred_element_type=F32))

    @pl.when(pl.program_id(1) == 0)
    def _():
        _round_into(wab, [wa_ref])
        _round_into(wbb, [wb_ref])
        os_ref[...] = xs_ref[...] + mix(as_ref[...], bs_ref[...])

    op_ref[...] = xp_ref[...] + mix(ap_ref[...], bp_ref[...])


def out_proj_residual(ap, bp, as_, bs, w_all, idx, xp, xs):
    mp, ka = ap.shape
    ms = as_.shape[0]
    assert bp.shape[1] == ka and w_all.shape[1] == 2 * ka
    n = w_all.shape[-1]
    bm, bn = min(mp, ROW_BLOCK), COL_BLOCK
    n_j, n_i = n // bn, mp // bm
    assert mp % bm == 0 and n % bn == 0
    row = lambda r, c: pl.BlockSpec((r, c), lambda j, i: (i, 0))
    fix = lambda r, c: pl.BlockSpec((r, c), lambda j, i: (0, 0))
    wa_spec, wb_spec = _weight_specs((ka, ka), bn, 2, n_j, n_i, (idx,))
    return pl.pallas_call(
        _out_proj_kernel,
        grid=(n_j, n_i),
        in_specs=[row(bm, ka), row(bm, ka), fix(ms, ka), fix(ms, ka), wa_spec, wb_spec,
                  pl.BlockSpec((bm, bn), lambda j, i: (i, j)),
                  pl.BlockSpec((ms, bn), lambda j, i: (0, j))],
        out_specs=[pl.BlockSpec((bm, bn), lambda j, i: (i, j)),
                   pl.BlockSpec((ms, bn), lambda j, i: (0, j))],
        out_shape=[jax.ShapeDtypeStruct((mp, n), F32), jax.ShapeDtypeStruct((ms, n), F32)],
        scratch_shapes=[pltpu.VMEM((ka, bn), BF16), pltpu.VMEM((ka, bn), BF16)],
        compiler_params=_params(("arbitrary", "arbitrary"), 54),
        name="out_proj",
    )(ap, bp, as_, bs, w_all, w_all, xp, xs)


def _cumsum_rows(x):
    n = x.shape[0]
    row = lax.broadcasted_iota(jnp.int32, x.shape, 0)
    s = 1
    while s < n:
        x = x + jnp.where(row >= s, pltpu.roll(x, s, 0), 0.0)
        s *= 2
    return x


def _linear_scan_rows(a, u):
    n = a.shape[0]
    row = lax.broadcasted_iota(jnp.int32, a.shape, 0)
    s = 1
    while s < n:
        keep = row >= s
        a_prev = jnp.where(keep, pltpu.roll(a, s, 0), 1.0)
        u_prev = jnp.where(keep, pltpu.roll(u, s, 0), 0.0)
        u = a * u_prev + u
        a = a * a_prev
        s *= 2
    return a, u


def _retention_kernel(q_ref, k_ref, v_ref, g_ref, cos_ref, sin_ref, lg_ref, gn_ref, s0_ref,
                      o_ref, sout_ref, s_scr, dec_scr, *, chunk, n_valid, dk):
    c = pl.program_id(2)
    log_g = lg_ref[0][:, :1]

    @pl.when(c == 0)
    def _():
        s_scr[...] = s0_ref[0, 0]
        i = lax.broadcasted_iota(jnp.int32, (chunk, chunk), 0)
        j = lax.broadcasted_iota(jnp.int32, (chunk, chunk), 1)
        d = (i - j).astype(F32) * log_g
        dec_scr[...] = jnp.exp(jnp.where(i >= j, d, -jnp.inf))

    half = dk // 2
    cos = cos_ref[...]
    sin = sin_ref[...]

    def rot(x):
        x1 = x[:, :half]
        x2 = x[:, half:]
        return jnp.concatenate([x1 * cos - x2 * sin, x1 * sin + x2 * cos], axis=-1)

    q = rot(q_ref[...])
    k = rot(k_ref[...]) * (dk ** -0.5)
    v = v_ref[...]
    idx = lax.broadcasted_iota(jnp.int32, (chunk, 1), 0).astype(F32)
    inter = jnp.exp((idx + 1.0) * log_g)
    to_state = jnp.exp((n_valid - 1.0 - idx) * log_g)
    carry = jnp.exp(n_valid * log_g)

    s = s_scr[...]
    att = _bdot_nt(q, k) * dec_scr[...]
    o = _bdot(att, v) + _bdot(q, s) * inter
    s_new = carry * s + _bdot_tn(k * to_state, v)
    s_scr[...] = s_new

    mu = jnp.mean(o, axis=-1, keepdims=True)
    oc = o - mu
    var = jnp.mean(oc * oc, axis=-1, keepdims=True)
    y = oc * lax.rsqrt(var + NORM_EPS) * gn_ref[...]
    o_ref[...] = (y * _silu(g_ref[...])).astype(o_ref.dtype)

    @pl.when(c == pl.num_programs(2) - 1)
    def _():
        sout_ref[0, 0] = s_new


def retention(proj, row0, b, t, cos, sin, gn, s0, *, n_heads, dk, dv, n_valid, chunk):
    assert dk == dv and t % chunk == 0 and row0 % chunk == 0 and (n_valid == chunk or t == chunk)
    h = n_heads
    nc = t // chunk
    rb0 = row0 // chunk
    log_g = jnp.log1p(-jnp.exp2(-5.0 - jnp.arange(h, dtype=F32)))
    lg = jnp.broadcast_to(log_g[:, None, None], (h, 1, 128))
    kern = functools.partial(_retention_kernel, chunk=chunk, n_valid=n_valid, dk=dk)
    col = lambda off: pl.BlockSpec((chunk, dk), lambda bi, hi, ci, off=off: (rb0 + bi * nc + ci, off + hi))
    return pl.pallas_call(
        kern,
        grid=(b, h, nc),
        in_specs=[col(0), col(h), col(2 * h), col(3 * h),
                  pl.BlockSpec((chunk, dk // 2), lambda bi, hi, ci: (ci, 0)),
                  pl.BlockSpec((chunk, dk // 2), lambda bi, hi, ci: (ci, 0)),
                  pl.BlockSpec((1, 1, 128), lambda bi, hi, ci: (hi, 0, 0)),
                  pl.BlockSpec((1, dv), lambda bi, hi, ci: (0, hi)),
                  pl.BlockSpec((1, 1, dk, dv), lambda bi, hi, ci: (bi, hi, 0, 0))],
        out_specs=[pl.BlockSpec((chunk, dv), lambda bi, hi, ci: (bi * nc + ci, hi)),
                   pl.BlockSpec((1, 1, dk, dv), lambda bi, hi, ci: (bi, hi, 0, 0))],
        out_shape=[jax.ShapeDtypeStruct((b * t, h * dv), BF16),
                   jax.ShapeDtypeStruct((b, h, dk, dv), F32)],
        scratch_shapes=[pltpu.VMEM((dk, dv), F32), pltpu.VMEM((chunk, chunk), F32)],
        compiler_params=_params(("parallel", "parallel", "arbitrary"), 32),
        name="retention",
    )(proj, proj, proj, proj, cos, sin, lg, gn.reshape(1, h * dv), s0)


def _moba_prompt_kernel(q_ref, k_ref, v_ref, o_ref, sel_scr, kb_scr, vb_scr, *, nb, scale, hp, dh):
    i = pl.program_id(2)
    blk = MOBA_BLOCK

    @pl.when(i == 0)
    def _():
        for k in range(hp):
            sl = slice(k * dh, (k + 1) * dh)
            kk = k_ref[:, sl]
            t = kk.shape[0]
            kb_scr[k] = kk.astype(BF16)
            vb_scr[k] = v_ref[:, sl].astype(BF16)
            kmean = jnp.mean(kk.reshape(nb, blk, dh), axis=1)
            gate = lax.dot_general(kmean, q_ref[:, sl], _NT, precision=lax.Precision.HIGHEST,
                                   preferred_element_type=F32)
            bidx = lax.broadcasted_iota(jnp.int32, (nb, t), 0)
            past = bidx < lax.broadcasted_iota(jnp.int32, (nb, t), 1) // blk
            gate = jnp.where(past, gate, -jnp.inf)
            rank = jnp.zeros(gate.shape, jnp.int32)
            for mm in range(nb):
                gm = gate[mm:mm + 1, :]
                rank = rank + ((gm > gate) | ((gm == gate) & (bidx > mm))).astype(jnp.int32)
            sel_scr[k] = jnp.where((rank < MOBA_TOPK) & past, 1.0, 0.0).T

    own = pl.multiple_of(i * blk, blk)
    row = lax.broadcasted_iota(jnp.int32, (blk, blk), 0)
    colk = lax.broadcasted_iota(jnp.int32, (blk, blk), 1)
    causal = colk <= row

    def attend(nkb):
        outs = []
        for k in range(hp):
            qb = q_ref[pl.ds(own, blk), k * dh:(k + 1) * dh].astype(BF16)
            sel = sel_scr[k, pl.ds(own, blk), :]
            s = _bdot_nt(qb, kb_scr[k, :nkb * blk, :]) * scale
            pieces = []
            for n in range(nkb):
                is_own = jnp.full((blk, blk), n, jnp.int32) == i
                allowed = (jnp.broadcast_to(sel[:, n:n + 1], (blk, blk)) > 0.5) | (is_own & causal)
                pieces.append(jnp.where(allowed, s[:, n * blk:(n + 1) * blk], -jnp.inf))
            s = jnp.concatenate(pieces, axis=-1)
            m = jnp.max(s, axis=-1, keepdims=True)
            p = jnp.exp(s - m)
            l = jnp.sum(p, axis=-1, keepdims=True)
            outs.append(_bdot(p, vb_scr[k, :nkb * blk, :]) / l)
        o_ref[...] = jnp.concatenate(outs, axis=-1).astype(o_ref.dtype)

    step = 2 if nb % 2 == 0 else 1
    for nkb in range(step, nb + 1, step):
        @pl.when((i >= nkb - step) & (i < nkb))
        def _(nkb=nkb):
            attend(nkb)


def moba_prompt(proj, b, t, *, n_heads, dh, q_off, k_off, v_off):
    hp = MOBA_HEADS_PER_STEP
    assert t % MOBA_BLOCK == 0 and n_heads % hp == 0 and q_off % hp == 0 and k_off % hp == 0 and v_off % hp == 0
    nb = t // MOBA_BLOCK
    kern = functools.partial(_moba_prompt_kernel, nb=nb, scale=dh ** -0.5, hp=hp, dh=dh)
    w = hp * dh
    return pl.pallas_call(
        kern,
        grid=(b, n_heads // hp, nb),
        in_specs=[pl.BlockSpec((t, w), lambda bi, hi, ii: (bi, q_off // hp + hi)),
                  pl.BlockSpec((t, w), lambda bi, hi, ii: (bi, k_off // hp + hi)),
                  pl.BlockSpec((t, w), lambda bi, hi, ii: (bi, v_off // hp + hi))],
        out_specs=pl.BlockSpec((MOBA_BLOCK, w), lambda bi, hi, ii: (bi * nb + ii, hi)),
        out_shape=jax.ShapeDtypeStruct((b * t, n_heads * dh), BF16),
        scratch_shapes=[pltpu.VMEM((hp, t, nb), F32), pltpu.VMEM((hp, t, dh), BF16), pltpu.VMEM((hp, t, dh), BF16)],
        compiler_params=_params(("parallel", "parallel", "arbitrary"), 48),
        name="moba_prompt",
    )(proj, proj, proj)


def _moba_decode_kernel(pt_ref, q_ref, knew_ref, vnew_ref, *refs, n_heads, nblk, ts, ppb, scale):
    k_refs, v_refs = refs[:ppb], refs[ppb:2 * ppb]
    o_ref, km_scr, bias_scr, mb_scr, lb_scr, accb_scr = refs[2 * ppb:]
    n = pl.program_id(1)
    q = q_ref[0]
    rows, dh = q.shape
    qb = q.astype(BF16)

    @pl.when(n == 0)
    def _():
        r = lax.broadcasted_iota(jnp.int32, bias_scr.shape, 0)
        c = lax.broadcasted_iota(jnp.int32, bias_scr.shape, 1)
        bias_scr[...] = jnp.where((r % n_heads) == (c % n_heads), 0.0, -jnp.inf)

    ksum = jnp.sum(k_refs[0][...], axis=0)
    for k_ref in k_refs[1:]:
        ksum = ksum + jnp.sum(k_ref[...], axis=0)
    km_scr[n] = ksum * (1.0 / MOBA_BLOCK)

    bias = bias_scr[...]
    scores = [_bdot_nt(qb, k_ref[...].reshape(-1, dh)) * scale + bias for k_ref in k_refs]
    m_n = jnp.max(scores[0], axis=-1, keepdims=True)
    for s in scores[1:]:
        m_n = jnp.maximum(m_n, jnp.max(s, axis=-1, keepdims=True))
    l_n = jnp.zeros((rows, 1), F32)
    acc_n = jnp.zeros((rows, dh), F32)
    for s, v_ref in zip(scores, v_refs):
        e = jnp.exp(s - m_n)
        l_n = l_n + jnp.sum(e, axis=-1, keepdims=True)
        acc_n = acc_n + _bdot(e, v_ref[...].reshape(-1, dh))
    mb_scr[n] = m_n
    lb_scr[n] = l_n
    accb_scr[n] = acc_n

    @pl.when(n == nblk - 1)
    def _():
        km = km_scr[...].reshape(nblk * n_heads, dh)
        g = lax.dot_general(q, km, _NT, precision=lax.Precision.HIGHEST, preferred_element_type=F32)
        width = nblk * n_heads
        rs = lax.broadcasted_iota(jnp.int32, (rows, width), 0)
        cs = lax.broadcasted_iota(jnp.int32, (rows, width), 1)
        nidx = cs // n_heads
        rank = jnp.zeros((rows, width), jnp.int32)
        for k in range(1, nblk):
            other = pltpu.roll(g, k * n_heads, 1)
            rank = rank + ((other > g) | ((other == g) & (nidx >= k))).astype(jnp.int32)
        sel = jnp.where((rank < MOBA_TOPK) & ((cs % n_heads) == (rs % n_heads)), 1.0, 0.0)
        flags = [jnp.sum(sel[:, blk * n_heads:(blk + 1) * n_heads], axis=-1, keepdims=True) > 0.5
                 for blk in range(nblk)]

        kn = knew_ref[0].reshape(-1, dh)
        vn = vnew_ref[0].reshape(-1, dh)
        r = lax.broadcasted_iota(jnp.int32, (rows, kn.shape[0]), 0)
        c = lax.broadcasted_iota(jnp.int32, (rows, kn.shape[0]), 1)
        tj = c // n_heads
        ok = ((r % n_heads) == (c % n_heads)) & (tj <= r // n_heads) & (tj < ts)
        s = jnp.where(ok, _bdot_nt(qb, kn) * scale, -jnp.inf)
        m_own = jnp.max(s, axis=-1, keepdims=True)

        m = m_own
        for blk in range(nblk):
            m = jnp.maximum(m, jnp.where(flags[blk], mb_scr[blk], -jnp.inf))
        e = jnp.exp(s - m)
        l = jnp.sum(e, axis=-1, keepdims=True)
        acc = _bdot(e, vn)
        for blk in range(nblk):
            w = jnp.where(flags[blk], jnp.exp(mb_scr[blk] - m), 0.0)
            l = l + w * lb_scr[blk]
            acc = acc + w * accb_scr[blk]
        o_ref[0] = acc / l


def moba_sample(qb, knew, vnew, cache_k, cache_v, layer, page_table, *, n_heads, dh):
    b, ts, w = qb.shape
    n_pages = page_table.shape[1]
    page = cache_k.shape[2]
    past = n_pages * page
    assert past % MOBA_BLOCK == 0 and MOBA_BLOCK % page == 0 and ts <= MOBA_BLOCK and w == n_heads * dh
    ppb = MOBA_BLOCK // page
    nblk = past // MOBA_BLOCK
    rows = ts * n_heads
    ts_pad = -(-ts // 8) * 8
    pad = ((0, 0), (0, ts_pad - ts), (0, 0), (0, 0))
    q3 = qb.reshape(b, rows, dh)
    kn = jnp.pad(knew.reshape(b, ts, n_heads, dh), pad)
    vn = jnp.pad(vnew.reshape(b, ts, n_heads, dh), pad)
    page_specs = [pl.BlockSpec((None, None, page, n_heads, dh),
                               lambda bi, ni, pt, pg=pg: (layer, pt[bi, ni * ppb + pg], 0, 0, 0))
                  for pg in range(ppb)]

    out = pl.pallas_call(
        functools.partial(_moba_decode_kernel, n_heads=n_heads, nblk=nblk, ts=ts, ppb=ppb, scale=dh ** -0.5),
        grid_spec=pltpu.PrefetchScalarGridSpec(
            num_scalar_prefetch=1,
            grid=(b, nblk),
            in_specs=[pl.BlockSpec((1, rows, dh), lambda bi, ni, pt: (bi, 0, 0)),
                      pl.BlockSpec((1, ts_pad, n_heads, dh), lambda bi, ni, pt: (bi, 0, 0, 0)),
                      pl.BlockSpec((1, ts_pad, n_heads, dh), lambda bi, ni, pt: (bi, 0, 0, 0))]
                     + page_specs + page_specs,
            out_specs=pl.BlockSpec((1, rows, dh), lambda bi, ni, pt: (bi, 0, 0)),
            scratch_shapes=[pltpu.VMEM((nblk, n_heads, dh), F32), pltpu.VMEM((rows, page * n_heads), F32),
                            pltpu.VMEM((nblk, rows, 1), F32), pltpu.VMEM((nblk, rows, 1), F32),
                            pltpu.VMEM((nblk, rows, dh), F32)]),
        out_shape=jax.ShapeDtypeStruct((b, rows, dh), F32),
        compiler_params=_params(("parallel", "arbitrary"), 40),
        name="moba_sample_attn",
    )(page_table, q3, kn, vn, *([cache_k] * ppb), *([cache_v] * ppb))
    return out.reshape(b, ts, w)


def _rglru_kernel(x_ref, g_ref, tail0_ref, h0_ref, cw_ref, cb_ref, wr_ref, br_ref, wi_ref, bi_ref, lam_ref,
                  o_ref, hlast_ref, tail_scr, h_scr, *, tc, n_valid):
    c = pl.program_id(2)

    @pl.when(c == 0)
    def _():
        tail_scr[...] = tail0_ref[0]
        h_scr[...] = h0_ref[0]

    x = x_ref[...]
    cw = cw_ref[...]
    taps = cw.shape[0]
    ext = jnp.concatenate([tail_scr[...], x], axis=0)
    xc = cb_ref[...] + x * cw[taps - 1:taps, :]
    for s in range(1, taps):
        xc = xc + pltpu.roll(ext, s, 0)[8:, :] * cw[taps - 1 - s:taps - s, :]
    tail_scr[...] = x[tc - 8:, :]

    nblk = x.shape[1] // GATE_BLOCK
    zr, zi = [], []
    for kb in range(nblk):
        xb = xc[:, kb * GATE_BLOCK:(kb + 1) * GATE_BLOCK]
        zr.append(_bdot(xb, wr_ref[kb]))
        zi.append(_bdot(xb, wi_ref[kb]))
    r = jax.nn.sigmoid(jnp.concatenate(zr, axis=-1) + br_ref[...])
    ig = jax.nn.sigmoid(jnp.concatenate(zi, axis=-1) + bi_ref[...])
    lam = lam_ref[...]
    softplus_neg = jnp.maximum(-lam, 0.0) + jnp.log1p(jnp.exp(-jnp.abs(lam)))
    log_a = -RG_C * r * softplus_neg
    a = jnp.exp(log_a)
    u = jnp.sqrt(jnp.tanh(-log_a) * (1.0 + a * a)) * ig * xc
    a_cum, hz = _linear_scan_rows(a, u)
    h = hz + a_cum * h_scr[...]
    h_scr[...] = h[n_valid - 1:n_valid, :]
    o_ref[...] = (h * jax.nn.gelu(g_ref[...])).astype(o_ref.dtype)

    @pl.when(c == pl.num_programs(2) - 1)
    def _():
        hlast_ref[0] = h[n_valid - 1:n_valid, :]


def rglru(proj, row0, b, t, conv_buf, h0, conv_w, conv_b, w_r, b_r, w_i, b_i, lam, *, width, n_valid, tc, bw=512):
    taps = conv_w.shape[0]
    assert t % tc == 0 and tc % 8 == 0 and row0 % tc == 0 and (n_valid == tc or t == tc) and taps - 1 <= 8
    nw = width // bw
    gpb = bw // GATE_BLOCK
    nc = t // tc
    rb0 = row0 // tc
    tail0 = jnp.pad(conv_buf, ((0, 0), (8 - (taps - 1), 0), (0, 0)))
    vec = lambda a: a.reshape(1, width)
    vspec = pl.BlockSpec((1, bw), lambda bi, wi, ci: (0, wi))
    kern = functools.partial(_rglru_kernel, tc=tc, n_valid=n_valid)
    return pl.pallas_call(
        kern,
        grid=(b, nw, nc),
        in_specs=[pl.BlockSpec((tc, bw), lambda bi, wi, ci: (rb0 + bi * nc + ci, wi)),
                  pl.BlockSpec((tc, bw), lambda bi, wi, ci: (rb0 + bi * nc + ci, nw + wi)),
                  pl.BlockSpec((1, 8, bw), lambda bi, wi, ci: (bi, 0, wi)),
                  pl.BlockSpec((1, 1, bw), lambda bi, wi, ci: (bi, 0, wi)),
                  pl.BlockSpec((taps, bw), lambda bi, wi, ci: (0, wi)),
                  vspec,
                  pl.BlockSpec((gpb, GATE_BLOCK, GATE_BLOCK), lambda bi, wi, ci: (wi, 0, 0)),
                  vspec,
                  pl.BlockSpec((gpb, GATE_BLOCK, GATE_BLOCK), lambda bi, wi, ci: (wi, 0, 0)),
                  vspec, vspec],
        out_specs=[pl.BlockSpec((tc, bw), lambda bi, wi, ci: (bi * nc + ci, wi)),
                   pl.BlockSpec((1, 1, bw), lambda bi, wi, ci: (bi, 0, wi))],
        out_shape=[jax.ShapeDtypeStruct((b * t, width), BF16),
                   jax.ShapeDtypeStruct((b, 1, width), F32)],
        scratch_shapes=[pltpu.VMEM((8, bw), F32), pltpu.VMEM((1, bw), F32)],
        compiler_params=_params(("parallel", "parallel", "arbitrary"), 32),
        name="rglru",
    )(proj, proj, tail0, h0.reshape(b, 1, width), conv_w, vec(conv_b), w_r, vec(b_r), w_i, vec(b_i), vec(lam))


def _hgrn_head(q, fz, v, g, lb, ng, st, *, chunk, n_valid):
    dk = q.shape[1]
    log_f = jnp.log(lb + (1.0 - lb) * jax.nn.sigmoid(fz))
    key = (1.0 - lb) * jax.nn.sigmoid(-fz)
    if n_valid < chunk:
        live = lax.broadcasted_iota(jnp.int32, (chunk, dk), 0) < n_valid
        log_f = jnp.where(live, log_f, 0.0)
        key = jnp.where(live, key, 0.0)
    cum = _cumsum_rows(log_f)
    last = cum[chunk - 1:chunk, :]

    o = _bdot_nt(q * jnp.exp(cum), st)

    row = lax.broadcasted_iota(jnp.int32, (chunk, dk), 0)
    ri = lax.broadcasted_iota(jnp.int32, (chunk, chunk), 0)
    ci = lax.broadcasted_iota(jnp.int32, (chunk, chunk), 1)
    att = None
    s = chunk // 2
    while s >= SUB:
        grp = chunk // (2 * s)
        edge = jnp.broadcast_to(cum.reshape(grp, 2 * s, dk)[:, s - 1:s, :], (grp, 2 * s, dk)).reshape(chunk, dk)
        upper = (row % (2 * s)) >= s
        qs = q * jnp.exp(jnp.where(upper, cum - edge, -jnp.inf))
        ks = key * jnp.exp(jnp.where(upper, -jnp.inf, edge - cum))
        a = _bdot_nt(qs, ks)
        if grp > 1:
            a = jnp.where((ri // (2 * s)) == (ci // (2 * s)), a, 0.0)
        att = a if att is None else att + a
        s //= 2
    if att is not None:
        o = o + _bdot(att, v)

    nsb = chunk // SUB
    q3 = q.reshape(nsb, SUB, dk)
    k3 = key.reshape(nsb, SUB, dk)
    c3 = cum.reshape(nsb, SUB, dk)
    v3 = v.reshape(nsb, SUB, v.shape[1])
    ri = lax.broadcasted_iota(jnp.int32, (nsb, SUB, dk), 1)
    od = jnp.zeros(v3.shape, F32)
    for j in range(SUB):
        d = jnp.where(ri >= j, c3 - c3[:, j:j + 1, :], -jnp.inf)
        wgt = jnp.sum(q3 * jnp.exp(d) * k3[:, j:j + 1, :], axis=-1, keepdims=True)
        od = od + wgt * v3[:, j:j + 1, :]
    o = o + od.reshape(chunk, v.shape[1])

    st_new = st * jnp.exp(last) + _bdot_tn(v, key * jnp.exp(last - cum))
    y = o * lax.rsqrt(jnp.mean(o * o, axis=-1, keepdims=True) + NORM_EPS) * ng
    return y * _silu(g), st_new


def _hgrn_kernel(q_ref, f_ref, i_ref, g_ref, lbl_ref, ng_ref, s0_ref, o_ref, sout_ref, st_scr,
                 *, chunk, n_valid, layer, hp, dk):
    c = pl.program_id(2)

    @pl.when(c == 0)
    def _():
        for k in range(hp):
            st_scr[k] = s0_ref[0, k].T

    lbl = lbl_ref[...]
    e = jnp.exp(lbl - jnp.max(lbl, axis=0, keepdims=True))
    soft = e / jnp.sum(e, axis=0, keepdims=True)
    lb_all = jnp.zeros((1, lbl.shape[1]), F32)
    for r in range(1, layer + 1):
        lb_all = lb_all + soft[r:r + 1, :]

    outs, states = [], []
    for k in range(hp):
        sl = slice(k * dk, (k + 1) * dk)
        y, st_new = _hgrn_head(q_ref[:, sl], f_ref[:, sl], i_ref[:, sl], g_ref[:, sl], lb_all[:, sl],
                               ng_ref[:, sl], st_scr[k], chunk=chunk, n_valid=n_valid)
        st_scr[k] = st_new
        outs.append(y)
        states.append(st_new)
    o_ref[...] = jnp.concatenate(outs, axis=-1).astype(o_ref.dtype)

    @pl.when(c == pl.num_programs(2) - 1)
    def _():
        for k in range(hp):
            sout_ref[0, k] = states[k].T


def hgrn2(proj, row0, b, t, lb_logits, norm_g, s0, *, layer, n_heads, dk, dv, col0, n_valid, chunk=128):
    hp = HGRN_HEADS_PER_STEP
    assert dk == dv and t % chunk == 0 and chunk % SUB == 0 and row0 % chunk == 0 and (n_valid == chunk or t == chunk)
    assert n_heads % hp == 0 and col0 % hp == 0
    h = n_heads
    nc = t // chunk
    rb0 = row0 // chunk
    kern = functools.partial(_hgrn_kernel, chunk=chunk, n_valid=n_valid, layer=layer, hp=hp, dk=dk)
    col = lambda off: pl.BlockSpec((chunk, hp * dk),
                                   lambda bi, hi, ci, off=off: (rb0 + bi * nc + ci, (col0 + off) // hp + hi))
    nl = lb_logits.shape[0]
    return pl.pallas_call(
        kern,
        grid=(b, h // hp, nc),
        in_specs=[col(0), col(h), col(2 * h), col(3 * h),
                  pl.BlockSpec((nl, hp * dk), lambda bi, hi, ci: (0, hi)),
                  pl.BlockSpec((1, hp * dv), lambda bi, hi, ci: (0, hi)),
                  pl.BlockSpec((1, hp, dk, dv), lambda bi, hi, ci: (bi, hi, 0, 0))],
        out_specs=[pl.BlockSpec((chunk, hp * dv), lambda bi, hi, ci: (bi * nc + ci, hi)),
                   pl.BlockSpec((1, hp, dk, dv), lambda bi, hi, ci: (bi, hi, 0, 0))],
        out_shape=[jax.ShapeDtypeStruct((b * t, h * dv), BF16),
                   jax.ShapeDtypeStruct((b, h, dk, dv), F32)],
        scratch_shapes=[pltpu.VMEM((hp, dv, dk), F32)],
        compiler_params=_params(("parallel", "parallel", "arbitrary"), 32),
        name="hgrn2",
    )(proj, proj, proj, proj, lb_logits, norm_g.reshape(1, h * dv), s0)


def _rope_tables(pos, half):
    freq = ROPE_BASE ** (-jnp.arange(half, dtype=F32) / half)
    ang = pos.astype(F32)[:, None] * freq[None, :]
    return jnp.cos(ang), jnp.sin(ang)


def _pad_seq(x, b, t, t_pad):
    c = x.shape[1]
    return jnp.pad(x.reshape(b, t, c), ((0, 0), (0, t_pad - t), (0, 0))).reshape(b * t_pad, c)


def _unpad_seq(x, b, t, t_pad):
    return x.reshape(b, t_pad, x.shape[1])[:, :t].reshape(b * t, x.shape[1])


def kernel(x_prompt, x_sample, state_ret, cache_k, cache_v, state_rglru, state_conv, state_hgrn, page_table, norm_ffn1, norm_mix, norm_ffn2, ffn_gate, ffn_up, ffn_down, w_in_ab, w_out_ab, gn_ret, w_in_cd, w_out_cd, conv_w, conv_b, w_rgate, b_rgate, w_igate, b_igate, lru_lambda, hgrn_lb_logits, hgrn_norm, final_norm):
    depth = norm_ffn1.shape[0]
    d_model = x_prompt.shape[-1]
    bp, tp = x_prompt.shape[0], x_prompt.shape[1]
    bs, ts = x_sample.shape[0], x_sample.shape[1]
    mp, ms = bp * tp, bs * ts
    n_ab, _, h_a, dk_a, dv_a = state_ret.shape
    h_b, dh_b = cache_k.shape[3], cache_k.shape[4]
    n_cd, _, w_c = state_rglru.shape
    _, _, h_d, dk_d, dv_d = state_hgrn.shape
    taps = conv_w.shape[1]
    past_len = page_table.shape[1] * cache_k.shape[2]
    a_cols = h_a * dk_a
    b_cols = h_b * dh_b
    b_off = 4 * a_cols // dh_b

    wg, wu, wd = ffn_gate, ffn_up, ffn_down
    win_ab, wout_ab, win_cd, wout_cd = w_in_ab, w_out_ab, w_in_cd, w_out_cd

    ts_ret, ts_lru, ts_hg = 128, 8, 128
    ret_chunk_p = min(tp, 256)
    cos_p, sin_p = _rope_tables(jnp.arange(tp), dk_a // 2)
    cos_s, sin_s = _rope_tables(past_len + jnp.arange(ts_ret), dk_a // 2)
    zeros = lambda *s: jnp.zeros(s, F32)

    def ffn(xp, xs, l, j2, g):
        hp, hs = gate_up(rmsnorm(xp, g, BF16), rmsnorm(xs, g, BF16), wg, wu, l, j2)
        return down_residual(hp, hs, wd, l, j2, xp, xs, 0.5)

    xp = x_prompt.reshape(mp, d_model)
    xs = x_sample.reshape(ms, d_model)
    ret_p, ret_s, k_p, k_s, v_p, v_s = [], [], [], [], [], []
    h_p, h_s, buf_p, buf_s, hg_p, hg_s = [], [], [], [], [], []
    for l in range(depth):
        i = l // 2
        xp, xs = ffn(xp, xs, l, 0, norm_ffn1[l])
        hn_p = rmsnorm(xp, norm_mix[l], BF16)
        hn_s = rmsnorm(xs, norm_mix[l], BF16)
        if l % 2 == 0:
            proj, proj_s = in_proj(hn_p, hn_s, win_ab, i, F32)
            kb_cols = slice(4 * a_cols + b_cols, 4 * a_cols + 2 * b_cols)
            vb_cols = slice(4 * a_cols + 2 * b_cols, 4 * a_cols + 3 * b_cols)
            k_p.append(proj[:, kb_cols].reshape(bp, tp, h_b, dh_b))
            v_p.append(proj[:, vb_cols].reshape(bp, tp, h_b, dh_b))
            k_s.append(proj_s[:, kb_cols].reshape(bs, ts, h_b, dh_b))
            v_s.append(proj_s[:, vb_cols].reshape(bs, ts, h_b, dh_b))

            oa_p, s_p = retention(proj, 0, bp, tp, cos_p, sin_p, gn_ret[i], zeros(bp, h_a, dk_a, dv_a),
                                  n_heads=h_a, dk=dk_a, dv=dv_a, n_valid=ret_chunk_p, chunk=ret_chunk_p)
            oa_s, s_s = retention(_pad_seq(proj_s[:, :4 * a_cols], bs, ts, ts_ret), 0, bs, ts_ret, cos_s, sin_s,
                                  gn_ret[i], state_ret[i], n_heads=h_a, dk=dk_a, dv=dv_a, n_valid=ts, chunk=ts_ret)
            ob_p = moba_prompt(proj, bp, tp, n_heads=h_b, dh=dh_b, q_off=b_off, k_off=b_off + h_b, v_off=b_off + 2 * h_b)
            ob_s = moba_sample(proj_s[:, 4 * a_cols:4 * a_cols + b_cols].reshape(bs, ts, b_cols),
                               proj_s[:, kb_cols].reshape(bs, ts, b_cols), proj_s[:, vb_cols].reshape(bs, ts, b_cols),
                               cache_k, cache_v, i, page_table, n_heads=h_b, dh=dh_b)
            xp, xs = out_proj_residual(oa_p, ob_p, _unpad_seq(oa_s, bs, ts, ts_ret),
                                       ob_s.reshape(ms, b_cols).astype(BF16), wout_ab, i, xp, xs)
            ret_p.append(s_p)
            ret_s.append(s_s)
        else:
            proj, proj_s = in_proj(hn_p, hn_s, win_cd, i, F32)
            xb_p = proj[:, :w_c].reshape(bp, tp, w_c)
            xb_s = proj_s[:, :w_c].reshape(bs, ts, w_c)
            buf_p.append(jnp.concatenate([zeros(bp, taps - 1, w_c), xb_p], axis=1)[:, tp:])
            buf_s.append(jnp.concatenate([state_conv[i], xb_s], axis=1)[:, ts:])

            lru = functools.partial(rglru, conv_w=conv_w[i], conv_b=conv_b[i], w_r=w_rgate[i], b_r=b_rgate[i],
                                    w_i=w_igate[i], b_i=b_igate[i], lam=lru_lambda[i], width=w_c)
            oc_p, hl_p = lru(proj, 0, bp, tp, zeros(bp, taps - 1, w_c), zeros(bp, w_c), n_valid=min(tp, 256), tc=min(tp, 256))
            oc_s, hl_s = lru(_pad_seq(proj_s[:, :2 * w_c], bs, ts, ts_lru), 0, bs, ts_lru, state_conv[i], state_rglru[i],
                             n_valid=ts, tc=ts_lru)
            hg = functools.partial(hgrn2, lb_logits=hgrn_lb_logits, norm_g=hgrn_norm[i], layer=i, n_heads=h_d,
                                   dk=dk_d, dv=dv_d, col0=2 * w_c // dk_d, chunk=128)
            od_p, sg_p = hg(proj, 0, bp, tp, s0=zeros(bp, h_d, dk_d, dv_d), n_valid=128)
            od_s, sg_s = hg(_pad_seq(proj_s, bs, ts, ts_hg), 0, bs, ts_hg, s0=state_hgrn[i], n_valid=ts)
            xp, xs = out_proj_residual(oc_p, od_p, _unpad_seq(oc_s, bs, ts, ts_lru), _unpad_seq(od_s, bs, ts, ts_hg),
                                       wout_cd, i, xp, xs)
            h_p.append(hl_p.reshape(bp, w_c))
            h_s.append(hl_s.reshape(bs, w_c))
            hg_p.append(sg_p)
            hg_s.append(sg_s)
        xp, xs = ffn(xp, xs, l, 1, norm_ffn2[l])
    y_p = rmsnorm(xp, final_norm, F32)
    y_s = rmsnorm(xs, final_norm, F32)
    st = jnp.stack
    return (y_p.reshape(bp, tp, d_model), y_s.reshape(bs, ts, d_model),
            st(ret_p), st(k_p), st(v_p), st(h_p), st(buf_p), st(hg_p),
            st(ret_s), st(k_s), st(v_s), st(h_s), st(buf_s), st(hg_s))
```
